```python
import math
import jax, jax.numpy as jnp
from jax import lax
import numpy as np

D_MODEL = 1024
BATCH = 16
SEQ = 256
DEPTH = 4
DEC_BATCH = 2
DEC_SEQ = 4096
PAST_LEN = 256

GRID_W = 64
EPS = 1e-6
H_Q = 8
H_KV = 2
HEAD_DIM = 128
ATTN_W = H_Q * HEAD_DIM
KV_W = H_KV * HEAD_DIM
ROPE_THETA = 10000.0
Q_BLOCK = 128
D_INNER = 2 * D_MODEL
SSM_HEAD_DIM = 64
SSM_HEADS = D_INNER // SSM_HEAD_DIM
SSM_GROUPS = 4
SSM_STATE = 128
SSM_CONV = 3
SSM_CHUNK = 128
CONV_CH = D_INNER + 2 * SSM_GROUPS * SSM_STATE
D_FF = 2816
FFN_CONV = 3
IN_SIZES = (ATTN_W, KV_W, KV_W, D_INNER, CONV_CH, 2 * SSM_HEADS, D_MODEL, D_MODEL)
IN_W = ATTN_W + 2 * KV_W + D_INNER + CONV_CH + 2 * SSM_HEADS + 2 * D_MODEL

kernel_name = 'hybrid_dit_gqa_ssd_convffn_step'


def rms_norm(x):
    xf = x.astype(jnp.float32)
    return (xf * lax.rsqrt(jnp.mean(xf * xf, axis=-1, keepdims=True) + EPS)).astype(x.dtype)


def modulate(x, shift, scale):
    return rms_norm(x) * (1.0 + scale) + shift


def dw_conv(x, w, b):
    width = w.shape[0]
    pad = width // 2
    seq = x.shape[1]
    xp = jnp.pad(x, ((0, 0), (pad, width - 1 - pad), (0, 0)))
    out = b
    for i in range(width):
        out = out + xp[:, i:i + seq] * w[i]
    return out


def axial_rope_tables(seq):
    rows = seq // GRID_W
    row = jnp.repeat(jnp.arange(rows), GRID_W).astype(jnp.float32)
    col = jnp.tile(jnp.arange(GRID_W), rows).astype(jnp.float32)
    half = HEAD_DIM // 2
    inv_freq = ROPE_THETA ** (-jnp.arange(0, half, 2, dtype=jnp.float32) / half)
    ang_r = row[:, None] * inv_freq
    ang_c = col[:, None] * inv_freq
    ang = jnp.concatenate([ang_r, ang_r, ang_c, ang_c], axis=-1)
    return jnp.cos(ang), jnp.sin(ang)


def rotate_half(u):
    u1, u2 = jnp.split(u, 2, axis=-1)
    return jnp.concatenate([-u2, u1], axis=-1)


def apply_axial_rope(x, cos, sin):
    xr, xc = jnp.split(x, 2, axis=-1)
    xrot = jnp.concatenate([rotate_half(xr), rotate_half(xc)], axis=-1)
    return (x * cos[None, :, None, :] + xrot * sin[None, :, None, :]).astype(x.dtype)


def block_attention(q, k, v):
    b, lq = q.shape[0], q.shape[1]
    rep = H_Q // H_KV
    nb = lq // Q_BLOCK
    qb = jnp.moveaxis(q.reshape(b, nb, Q_BLOCK, H_KV, rep, HEAD_DIM), 1, 0)
    scale = HEAD_DIM ** -0.5

    def one_block(qi):
        s = jnp.einsum('bqgrd,bkgd->bgrqk', qi, k).astype(jnp.float32) * scale
        p = jax.nn.softmax(s, axis=-1).astype(v.dtype)
        return jnp.einsum('bgrqk,bkgd->bqgrd', p, v)

    o = lax.map(one_block, qb)
    return jnp.moveaxis(o, 0, 1).reshape(b, lq, ATTN_W)


def ssd_scan(x, dt, a, bm, cm, h0):
    b, seq = x.shape[0], x.shape[1]
    nc = seq // SSM_CHUNK
    qc = SSM_CHUNK
    g = SSM_GROUPS
    r = SSM_HEADS // SSM_GROUPS
    f32 = jnp.float32
    x = x.astype(f32).reshape(b, nc, qc, g, r, SSM_HEAD_DIM)
    dt = dt.astype(f32).reshape(b, nc, qc, g, r)
    bm = bm.astype(f32).reshape(b, nc, qc, g, SSM_STATE)
    cm = cm.astype(f32).reshape(b, nc, qc, g, SSM_STATE)
    cum = jnp.cumsum(dt * a.reshape(g, r), axis=2)
    tri = jnp.tril(jnp.ones((qc, qc), dtype=bool))[None, None, :, :, None, None]
    seg = cum[:, :, :, None] - cum[:, :, None]
    decay = jnp.exp(jnp.where(tri, seg, -jnp.inf))
    cb = jnp.einsum('bclgn,bcsgn->bclsg', cm, bm)
    wts = cb[..., None] * decay * dt[:, :, None]
    y_diag = jnp.einsum('bclsgr,bcsgrp->bclgrp', wts, x)
    decay_s = jnp.exp(cum[:, :, -1:] - cum)
    states = jnp.einsum('bcsgn,bcsgrp->bcgrpn', bm, (decay_s * dt)[..., None] * x)
    chunk_decay = jnp.exp(cum[:, :, -1])
    h_init = h0.astype(f32).reshape(b, g, r, SSM_HEAD_DIM, SSM_STATE)

    def step(h, inp):
        dec, st = inp
        return h * dec[..., None, None] + st, h

    h_last, h_prev = lax.scan(step, h_init, (jnp.moveaxis(chunk_decay, 1, 0), jnp.moveaxis(states, 1, 0)))
    h_prev = jnp.moveaxis(h_prev, 0, 1)
    y_off = jnp.einsum('bclgn,bcgrpn->bclgrp', cm, h_prev) * jnp.exp(cum)[..., None]
    y = (y_diag + y_off).reshape(b, seq, SSM_HEADS, SSM_HEAD_DIM)
    return y, h_last.reshape(b, SSM_HEADS, SSM_HEAD_DIM, SSM_STATE)


def ssd_branch(xbc, z, dt_raw, dt_bias, a_log, d_skip, norm_w, init):
    b, seq = xbc.shape[0], xbc.shape[1]
    gn = SSM_GROUPS * SSM_STATE
    xs, bm, cm = jnp.split(xbc, [D_INNER, D_INNER + gn], axis=-1)
    xh = xs.reshape(b, seq, SSM_HEADS, SSM_HEAD_DIM)
    bm = bm.reshape(b, seq, SSM_GROUPS, SSM_STATE)
    cm = cm.reshape(b, seq, SSM_GROUPS, SSM_STATE)
    dt = jax.nn.softplus(dt_raw.astype(jnp.float32).reshape(b, seq, 2, SSM_HEADS) + dt_bias)
    a = -jnp.exp(a_log.astype(jnp.float32))
    y_f, h_f = ssd_scan(xh, dt[:, :, 0], a[0], bm, cm, init[:, 0])
    y_b, h_b = ssd_scan(jnp.flip(xh, 1), jnp.flip(dt[:, :, 1], 1), a[1], jnp.flip(bm, 1), jnp.flip(cm, 1), init[:, 1])
    y = y_f + jnp.flip(y_b, 1) + d_skip[:, None] * xh.astype(jnp.float32)
    y = y.reshape(b, seq, D_INNER) * jax.nn.silu(z.astype(jnp.float32))
    y = rms_norm(y) * norm_w
    return y.astype(xbc.dtype), jnp.stack([h_f, h_b], axis=1).astype(xbc.dtype)


def trunk_layer(x, mod, rope, ctx_k, ctx_v, init_state, p):
    b, seq = x.shape[0], x.shape[1]
    sh1, sc1, g1, sh2, sc2, g2 = jnp.split(mod[:, None, :], 6, axis=-1)
    h = modulate(x, sh1, sc1)
    proj = h @ p['w_in']
    q, k, v, z, xbc, dt_raw, ga, gs = jnp.split(proj, list(np.cumsum(IN_SIZES)[:-1]), axis=-1)
    q = rms_norm(q.reshape(b, seq, H_Q, HEAD_DIM)) * p['q_norm']
    k = rms_norm(k.reshape(b, seq, H_KV, HEAD_DIM)) * p['k_norm']
    v = v.reshape(b, seq, H_KV, HEAD_DIM)
    if rope is None:
        k_all, v_all = k, v
    else:
        cos, sin = rope
        q = apply_axial_rope(q, cos, sin)
        k = apply_axial_rope(k, cos, sin)
        k_all = jnp.concatenate([ctx_k.astype(k.dtype), k], axis=1)
        v_all = jnp.concatenate([ctx_v.astype(v.dtype), v], axis=1)
    attn = block_attention(q, k_all, v_all)
    xbc = jax.nn.silu(dw_conv(xbc, p['conv_w'], p['conv_b']))
    ssd, h_final = ssd_branch(xbc, z, dt_raw, p['dt_bias'], p['a_log'], p['d_skip'], p['ssd_norm'], init_state)
    merged = jax.nn.sigmoid(ga) * (attn @ p['w_attn_o']) + jax.nn.sigmoid(gs) * (ssd @ p['w_ssd_o'])
    x = x + g1 * (merged @ p['w_out'])
    h = modulate(x, sh2, sc2)
    u = dw_conv(h @ p['w_up'], p['ffn_conv_w'], p['ffn_conv_b'])
    u_val, u_gate = jnp.split(u, 2, axis=-1)
    x = x + g2 * ((jax.nn.silu(u_gate) * u_val) @ p['w_down'])
    return x, k, v, h_final


def setup_inputs(seed: int = 0) -> dict:
    key = jax.random.key(seed)
    ks = jax.random.split(key, 26)
    f32 = jnp.float32

    def nrm(k, shape, s):
        return jax.random.normal(k, shape, f32) * s

    dt0 = jnp.exp(jax.random.uniform(ks[14], (DEPTH, 2, SSM_HEADS), f32, math.log(1e-3), math.log(1e-1)))
    return {
        'x_prompt': nrm(ks[0], (BATCH, SEQ, D_MODEL), 1.0),
        'x_sample': nrm(ks[1], (DEC_BATCH, DEC_SEQ, D_MODEL), 1.0),
        'c': nrm(ks[2], (DEC_BATCH, D_MODEL), 1.0),
        'cache_k': nrm(ks[3], (DEC_BATCH, DEPTH, PAST_LEN, H_KV, HEAD_DIM), 1.0),
        'cache_v': nrm(ks[4], (DEC_BATCH, DEPTH, PAST_LEN, H_KV, HEAD_DIM), 1.0),
        'state_ssd': nrm(ks[5], (DEC_BATCH, DEPTH, 2, SSM_HEADS, SSM_HEAD_DIM, SSM_STATE), 0.5),
        'c_ctx': nrm(ks[6], (D_MODEL,), 1.0),
        'w_mod': nrm(ks[7], (DEPTH, D_MODEL, 6 * D_MODEL), D_MODEL ** -0.5),
        'b_mod': nrm(ks[8], (DEPTH, 6 * D_MODEL), 0.01),
        'w_in': nrm(ks[9], (DEPTH, D_MODEL, IN_W), D_MODEL ** -0.5),
        'q_norm': 1.0 + nrm(ks[10], (DEPTH, HEAD_DIM), 0.02),
        'k_norm': 1.0 + nrm(ks[11], (DEPTH, HEAD_DIM), 0.02),
        'conv_w': nrm(ks[12], (DEPTH, SSM_CONV, CONV_CH), SSM_CONV ** -0.5),
        'conv_b': nrm(ks[13], (DEPTH, CONV_CH), 0.01),
        'dt_bias': dt0 + jnp.log(-jnp.expm1(-dt0)),
        'a_log': jnp.log(jax.random.uniform(ks[15], (DEPTH, 2, SSM_HEADS), f32, 1.0, 16.0)),
        'd_skip': 1.0 + nrm(ks[16], (DEPTH, SSM_HEADS), 0.02),
        'ssd_norm': 1.0 + nrm(ks[17], (DEPTH, D_INNER), 0.02),
        'w_attn_o': nrm(ks[18], (DEPTH, ATTN_W, D_MODEL), ATTN_W ** -0.5),
        'w_ssd_o': nrm(ks[19], (DEPTH, D_INNER, D_MODEL), D_INNER ** -0.5),
        'w_out': nrm(ks[20], (DEPTH, D_MODEL, D_MODEL), D_MODEL ** -0.5),
        'w_up': nrm(ks[21], (DEPTH, D_MODEL, 2 * D_FF), D_MODEL ** -0.5),
        'ffn_conv_w': nrm(ks[22], (DEPTH, FFN_CONV, 2 * D_FF), FFN_CONV ** -0.5),
        'ffn_conv_b': nrm(ks[23], (DEPTH, 2 * D_FF), 0.01),
        'w_down': nrm(ks[24], (DEPTH, D_FF, D_MODEL), D_FF ** -0.5),
    }


def reference(x_prompt, x_sample, c, cache_k, cache_v, state_ssd, c_ctx, w_mod, b_mod, w_in, q_norm, k_norm,
              conv_w, conv_b, dt_bias, a_log, d_skip, ssd_norm, w_attn_o, w_ssd_o, w_out, w_up,
              ffn_conv_w, ffn_conv_b, w_down):
    rope = axial_rope_tables(x_sample.shape[1])
    xp = x_prompt
    xs = x_sample
    zero_state = jnp.zeros((xp.shape[0], 2, SSM_HEADS, SSM_HEAD_DIM, SSM_STATE), xp.dtype)
    new_k, new_v, new_s = [], [], []
    for l in range(DEPTH):
        p = {
            'w_in': w_in[l], 'q_norm': q_norm[l], 'k_norm': k_norm[l],
            'conv_w': conv_w[l], 'conv_b': conv_b[l], 'dt_bias': dt_bias[l], 'a_log': a_log[l],
            'd_skip': d_skip[l], 'ssd_norm': ssd_norm[l], 'w_attn_o': w_attn_o[l], 'w_ssd_o': w_ssd_o[l],
            'w_out': w_out[l], 'w_up': w_up[l], 'ffn_conv_w': ffn_conv_w[l], 'ffn_conv_b': ffn_conv_b[l],
            'w_down': w_down[l],
        }
        mod_ctx = jax.nn.silu(c_ctx)[None, :] @ w_mod[l] + b_mod[l]
        mod_lat = jax.nn.silu(c) @ w_mod[l] + b_mod[l]
        xp, k_l, v_l, s_l = trunk_layer(xp, mod_ctx, None, None, None, zero_state, p)
        new_k.append(k_l)
        new_v.append(v_l)
        new_s.append(s_l)
        xs, _, _, _ = trunk_layer(xs, mod_lat, rope, cache_k[:, l], cache_v[:, l], state_ssd[:, l], p)
    new_cache_k = jnp.stack(new_k, axis=1)
    new_cache_v = jnp.stack(new_v, axis=1)
    new_state_ssd = jnp.stack(new_s, axis=1)
    return (xp, xs, new_cache_k, new_cache_v, new_state_ssd)
```

```python
import functools
import math

import numpy as np
import jax
import jax.numpy as jnp
from jax import lax
from jax.experimental import pallas as pl
from jax.experimental.pallas import tpu as pltpu

F32 = jnp.float32
BF16 = jnp.bfloat16

D_MODEL = 1024
GRID_W = 64
EPS = 1e-6
H_Q = 8
H_KV = 2
HEAD_DIM = 128
ATTN_W = H_Q * HEAD_DIM
KV_W = H_KV * HEAD_DIM
ROPE_THETA = 10000.0
D_INNER = 2 * D_MODEL
SSM_HEAD_DIM = 64
SSM_HEADS = D_INNER // SSM_HEAD_DIM
SSM_GROUPS = 4
SSM_STATE = 128
SSM_CHUNK = 128
CONV_CH = D_INNER + 2 * SSM_GROUPS * SSM_STATE
D_FF = 2816
GROUP_W = (SSM_HEADS // SSM_GROUPS) * SSM_HEAD_DIM

LANES = 128
SUBLANES_F32 = 8
SUBLANES_BF16 = 16
VMEM_LIMIT_BYTES = 56 * 1024 * 1024

Q0 = 0
K0 = Q0 + ATTN_W
V0 = K0 + KV_W
Z0 = V0 + KV_W
X0 = Z0 + D_INNER
DT0 = X0 + CONV_CH
GA0 = DT0 + 2 * LANES
GS0 = GA0 + D_MODEL
N_PACK = GS0 + D_MODEL

ROW_TILE = 256


def _sigmoid(x):
    return 1.0 / (1.0 + jnp.exp(-x))


def _silu(x):
    return x * _sigmoid(x)


def _params(n_axes):
    return pltpu.CompilerParams(dimension_semantics=("arbitrary",) * n_axes,
                                vmem_limit_bytes=VMEM_LIMIT_BYTES)


def _resident(shape, index_map):
    return pl.BlockSpec(shape, index_map, pipeline_mode=pl.Buffered(1))


def _mod_kernel(cond_ref, w_ref, b_ref, o_ref):
    a = _silu(cond_ref[...]).astype(BF16)
    o_ref[0] = jnp.dot(a, w_ref[0].astype(BF16), preferred_element_type=F32) + b_ref[0]


def _mod_call(cond, w_mod, b_mod):
    depth, _, n = w_mod.shape
    tn = 1536
    return pl.pallas_call(
        _mod_kernel,
        grid=(depth, n // tn),
        in_specs=[pl.BlockSpec(cond.shape, lambda l, j: (0, 0)),
                  pl.BlockSpec((1, D_MODEL, tn), lambda l, j: (l, 0, j)),
                  pl.BlockSpec((1, 1, tn), lambda l, j: (l, 0, j))],
        out_specs=pl.BlockSpec((1, cond.shape[0], tn), lambda l, j: (l, 0, j)),
        out_shape=jax.ShapeDtypeStruct((depth, cond.shape[0], n), F32),
        compiler_params=_params(2),
        name="mod_vectors",
    )(cond, w_mod, b_mod.reshape(depth, 1, n))


def _modulate(x, shift, scale):
    ms = jnp.mean(x * x, axis=-1, keepdims=True)
    return (x * lax.rsqrt(ms + EPS)) * (1.0 + scale) + shift


def _in_kernel(crow_ref, hprev_ref, hnext_ref, rblk_ref,
               x_ref, xp_ref, xn_ref, mod_ref, cos_ref, sin_ref, qn_ref, kn_ref, cw_ref, cb_ref,
               dtb_ref, w_ref,
               q_ref, k_ref, v_ref, kf_ref, vf_ref, z_ref, xbc_ref, dt_ref, ga_ref, gs_ref,
               s_ref):
    del rblk_ref
    i = pl.program_id(0)
    tm = x_ref.shape[0]
    halo = xp_ref.shape[0]
    mod = mod_ref[pl.ds(crow_ref[i], 1), :]
    shift = mod[:, 0:D_MODEL]
    scale = mod[:, D_MODEL:2 * D_MODEL]
    hm = _modulate(x_ref[...], shift, scale)
    hp = _modulate(xp_ref[...], shift, scale)
    hn = _modulate(xn_ref[...], shift, scale)
    h = hm.astype(BF16)
    h_ext = jnp.concatenate([hp, hm, hn], axis=0).astype(BF16)

    qkv = jnp.dot(h, w_ref[:, Q0:Z0], preferred_element_type=F32)
    cos = cos_ref[...]
    sin = sin_ref[...]
    lane = lax.broadcasted_iota(jnp.int32, (tm, HEAD_DIM), 1)
    first_quarter = (lane & (HEAD_DIM // 4)) == 0

    def norm_rope(u, w):
        un = u * lax.rsqrt(jnp.mean(u * u, axis=-1, keepdims=True) + EPS) * w
        rot = jnp.where(first_quarter, pltpu.roll(un, HEAD_DIM - HEAD_DIM // 4, 1),
                        pltpu.roll(un, HEAD_DIM // 4, 1))
        return un * cos + rot * sin

    q_scale = HEAD_DIM ** -0.5
    for j in range(H_Q):
        qh = norm_rope(qkv[:, j * HEAD_DIM:(j + 1) * HEAD_DIM], qn_ref[...])
        q_ref[:, j * HEAD_DIM:(j + 1) * HEAD_DIM] = (qh * q_scale).astype(BF16)
    for j in range(H_KV):
        kh = norm_rope(qkv[:, K0 + j * HEAD_DIM:K0 + (j + 1) * HEAD_DIM], kn_ref[...])
        k_ref[:, j * HEAD_DIM:(j + 1) * HEAD_DIM] = kh.astype(BF16)
        kf_ref[:, j * HEAD_DIM:(j + 1) * HEAD_DIM] = kh
    vv = qkv[:, V0:Z0]
    v_ref[...] = vv.astype(BF16)
    vf_ref[...] = vv

    z_ref[...] = jnp.dot(h, w_ref[:, Z0:X0], preferred_element_type=F32)

    s_ref[...] = jnp.dot(h_ext, w_ref[:, X0:DT0], preferred_element_type=F32)

    @pl.when(hprev_ref[i] == 0)
    def _():
        s_ref[0:halo, :] = jnp.zeros((halo, CONV_CH), F32)

    @pl.when(hnext_ref[i] == 0)
    def _():
        s_ref[halo + tm:2 * halo + tm, :] = jnp.zeros((halo, CONV_CH), F32)

    xc = (cb_ref[...] + cw_ref[0:1, :] * s_ref[halo - 1:halo - 1 + tm, :]
          + cw_ref[1:2, :] * s_ref[halo:halo + tm, :]
          + cw_ref[2:3, :] * s_ref[halo + 1:halo + 1 + tm, :])
    xbc_ref[...] = _silu(xc)

    dt_raw = jnp.dot(h, w_ref[:, DT0:GA0], preferred_element_type=F32) + dtb_ref[...]
    dt_ref[...] = jnp.maximum(dt_raw, 0.0) + jnp.log1p(jnp.exp(-jnp.abs(dt_raw)))

    ga_ref[...] = _sigmoid(jnp.dot(h, w_ref[:, GA0:GS0], preferred_element_type=F32))
    gs_ref[...] = _sigmoid(jnp.dot(h, w_ref[:, GS0:N_PACK], preferred_element_type=F32))


def _in_call(x, mod, tabs, rope_cos, rope_sin, q_norm, k_norm, conv_w, conv_b, dt_bias, w_pack):
    n_tok = x.shape[0]
    tm = ROW_TILE
    halo = SUBLANES_F32
    n_tiles = n_tok // tm
    hb = tm // halo
    last_hb = n_tok // halo - 1

    def row(width):
        return pl.BlockSpec((tm, width), lambda i, *_: (i, 0))

    def const(shape):
        return pl.BlockSpec(shape, lambda i, *_: (0,) * len(shape))

    grid_spec = pltpu.PrefetchScalarGridSpec(
        num_scalar_prefetch=4,
        grid=(n_tiles,),
        in_specs=[
            row(D_MODEL),
            pl.BlockSpec((halo, D_MODEL), lambda i, *_: (jnp.maximum(i * hb - 1, 0), 0)),
            pl.BlockSpec((halo, D_MODEL), lambda i, *_: (jnp.minimum((i + 1) * hb, last_hb), 0)),
            const(mod.shape),
            pl.BlockSpec((tm, HEAD_DIM), lambda i, c, p, n, r: (r[i], 0)),
            pl.BlockSpec((tm, HEAD_DIM), lambda i, c, p, n, r: (r[i], 0)),
            const((1, HEAD_DIM)), const((1, HEAD_DIM)),
            const((3, CONV_CH)), const((1, CONV_CH)), const((1, 2 * LANES)),
            _resident((D_MODEL, N_PACK), lambda i, *_: (0, 0)),
        ],
        out_specs=[row(ATTN_W), row(KV_W), row(KV_W), row(KV_W), row(KV_W), row(D_INNER),
                   row(CONV_CH), row(2 * LANES), row(D_MODEL), row(D_MODEL)],
        scratch_shapes=[pltpu.VMEM((tm + 2 * halo, CONV_CH), F32)],
    )
    sds = jax.ShapeDtypeStruct
    return pl.pallas_call(
        _in_kernel,
        grid_spec=grid_spec,
        out_shape=[sds((n_tok, ATTN_W), BF16), sds((n_tok, KV_W), BF16), sds((n_tok, KV_W), BF16),
                   sds((n_tok, KV_W), F32), sds((n_tok, KV_W), F32), sds((n_tok, D_INNER), F32),
                   sds((n_tok, CONV_CH), F32), sds((n_tok, 2 * LANES), F32),
                   sds((n_tok, D_MODEL), F32), sds((n_tok, D_MODEL), F32)],
        compiler_params=_params(1),
        name="in_proj",
    )(tabs["crow"], tabs["hprev"], tabs["hnext"], tabs["rblk"],
      x, x, x, mod, rope_cos, rope_sin, q_norm, k_norm, conv_w, conv_b, dt_bias, w_pack)


def _attn_kernel(*refs, has_cache):
    if has_cache:
        q_ref, k_ref, v_ref, kc_ref, vc_ref, _, o_ref = refs
    else:
        q_ref, k_ref, v_ref, o_ref = refs
    k = k_ref[...]
    v = v_ref[...]
    nt = (((1,), (1,)), ((), ()))
    if has_cache:
        kc = kc_ref[0].astype(BF16)
        vc = vc_ref[0].astype(BF16)
    for j in range(H_Q // H_KV):
        qh = q_ref[:, j * HEAD_DIM:(j + 1) * HEAD_DIM]
        s = lax.dot_general(qh, k, nt, preferred_element_type=F32)
        m = jnp.max(s, axis=-1, keepdims=True)
        if has_cache:
            sc = lax.dot_general(qh, kc, nt, preferred_element_type=F32)
            m = jnp.maximum(m, jnp.max(sc, axis=-1, keepdims=True))
        p = jnp.exp(s - m)
        den = jnp.sum(p, axis=-1, keepdims=True)
        o = jnp.dot(p.astype(BF16), v, preferred_element_type=F32)
        if has_cache:
            pc = jnp.exp(sc - m)
            den = den + jnp.sum(pc, axis=-1, keepdims=True)
            o = o + jnp.dot(pc.astype(BF16), vc, preferred_element_type=F32)
        o_ref[:, j * HEAD_DIM:(j + 1) * HEAD_DIM] = (o / den).astype(BF16)


def _attn_ctx_call(q, k, v, batch, seq):
    n_tok = q.shape[0]
    qw = ATTN_W // H_KV
    return pl.pallas_call(
        functools.partial(_attn_kernel, has_cache=False),
        grid=(batch, H_KV),
        in_specs=[pl.BlockSpec((seq, qw), lambda b, g: (b, g)),
                  pl.BlockSpec((seq, HEAD_DIM), lambda b, g: (b, g)),
                  pl.BlockSpec((seq, HEAD_DIM), lambda b, g: (b, g))],
        out_specs=pl.BlockSpec((seq, qw), lambda b, g: (b, g)),
        out_shape=jax.ShapeDtypeStruct((n_tok, ATTN_W), BF16),
        compiler_params=_params(2),
        name="attn_context",
    )(q, k, v)


def _attn_lat_call(q, k, v, cache_k, cache_v, attn, n_ctx_tok, dec_batch, dec_seq, layer):
    tq = ROW_TILE
    qw = ATTN_W // H_KV
    t0 = n_ctx_tok // tq
    s0 = n_ctx_tok // dec_seq
    tiles = dec_seq // tq
    past = cache_k.shape[2]
    return pl.pallas_call(
        functools.partial(_attn_kernel, has_cache=True),
        grid=(dec_batch, H_KV, tiles),
        in_specs=[pl.BlockSpec((tq, qw), lambda b, g, t: (t0 + b * tiles + t, g)),
                  pl.BlockSpec((dec_seq, HEAD_DIM), lambda b, g, t: (s0 + b, g)),
                  pl.BlockSpec((dec_seq, HEAD_DIM), lambda b, g, t: (s0 + b, g)),
                  pl.BlockSpec((None, 1, past, HEAD_DIM), lambda b, g, t: (b, layer, 0, g)),
                  pl.BlockSpec((None, 1, past, HEAD_DIM), lambda b, g, t: (b, layer, 0, g)),
                  pl.BlockSpec(memory_space=pl.ANY)],
        out_specs=pl.BlockSpec((tq, qw), lambda b, g, t: (t0 + b * tiles + t, g)),
        out_shape=jax.ShapeDtypeStruct(attn.shape, attn.dtype),
        input_output_aliases={5: 0},
        compiler_params=_params(3),
        name="attn_latent",
    )(q, k, v, cache_k, cache_v, attn)


def _ssd_kernel(chunk_ref, first_ref, last_ref, islat_ref, latb_ref, ctxb_ref,
                x_ref, dt_ref, alog_ref, init_ref, y_ref, fin_ref, h_ref, *, n_steps):
    del chunk_ref, latb_ref, ctxb_ref
    d = pl.program_id(0)
    step = d * n_steps + pl.program_id(1)
    q = SSM_CHUNK
    heads_per_group = SSM_HEADS // SSM_GROUPS

    @pl.when((first_ref[step] == 1) & (islat_ref[step] == 0))
    def _():
        h_ref[...] = jnp.zeros(h_ref.shape, F32)

    @pl.when((first_ref[step] == 1) & (islat_ref[step] == 1))
    def _():
        for g in range(SSM_GROUPS):
            h_ref[g] = init_ref[0, 0, 0, g * GROUP_W:(g + 1) * GROUP_W, :].T

    dt = dt_ref[...]
    a = -jnp.exp(alog_ref[0, 0])
    row = lax.broadcasted_iota(jnp.int32, (q, q), 0)
    col = lax.broadcasted_iota(jnp.int32, (q, q), 1)
    sign = 1 - 2 * d
    mask = (row - col) * sign >= 0
    ones_tri = jnp.where(mask, 1.0, 0.0).astype(F32)
    cum = jnp.dot(ones_tri, dt * a, precision=lax.Precision.HIGHEST, preferred_element_type=F32)
    cum_t = cum.T
    dt_t = dt.T
    is_fwd = d == 0
    total = jnp.where(is_fwd, cum[q - 1:q, :], cum[0:1, :])
    state_w = dt * jnp.exp(total - cum)
    lo = lax.broadcasted_iota(jnp.int32, (q, LANES), 1) < SSM_HEAD_DIM
    lo_row = lo[0:1, :]

    for g in range(SSM_GROUPS):
        b0 = D_INNER + g * SSM_STATE
        c0 = D_INNER + SSM_GROUPS * SSM_STATE + g * SSM_STATE
        bg_t = x_ref[:, b0:b0 + SSM_STATE].T.astype(BF16)
        cg = x_ref[:, c0:c0 + SSM_STATE].astype(BF16)
        cb = jnp.dot(cg, bg_t, preferred_element_type=F32)
        h_t = h_ref[g]
        y_off = jnp.dot(cg, h_t.astype(BF16), preferred_element_type=F32)
        xw_parts = []
        dec_parts = []
        for pp in range(heads_per_group // 2):
            h0 = g * heads_per_group + 2 * pp
            h1 = h0 + 1
            col0 = jnp.broadcast_to(cum[:, h0:h0 + 1], (q, q))
            col1 = jnp.broadcast_to(cum[:, h1:h1 + 1], (q, q))
            w0 = cb * jnp.exp(jnp.where(mask, col0 - cum_t[h0:h0 + 1, :], -jnp.inf)) * dt_t[h0:h0 + 1, :]
            w1 = cb * jnp.exp(jnp.where(mask, col1 - cum_t[h1:h1 + 1, :], -jnp.inf)) * dt_t[h1:h1 + 1, :]
            x0 = h0 * SSM_HEAD_DIM
            xpair = x_ref[:, x0:x0 + LANES]
            xm0 = jnp.where(lo, xpair, 0.0).astype(BF16)
            xm1 = jnp.where(lo, 0.0, xpair).astype(BF16)
            y_diag = (jnp.dot(w0.astype(BF16), xm0, preferred_element_type=F32)
                      + jnp.dot(w1.astype(BF16), xm1, preferred_element_type=F32))
            colsel = jnp.where(lo, col0, col1)
            y_ref[0, :, x0:x0 + LANES] = y_diag + y_off[:, pp * LANES:(pp + 1) * LANES] * jnp.exp(colsel)
            sw = jnp.where(lo, jnp.broadcast_to(state_w[:, h0:h0 + 1], (q, LANES)),
                           jnp.broadcast_to(state_w[:, h1:h1 + 1], (q, LANES)))
            xw_parts.append((xpair * sw).astype(BF16))
            tot0 = jnp.broadcast_to(total[:, h0:h0 + 1], (1, LANES))
            tot1 = jnp.broadcast_to(total[:, h1:h1 + 1], (1, LANES))
            dec_parts.append(jnp.exp(jnp.where(lo_row, tot0, tot1)))
        xw = jnp.concatenate(xw_parts, axis=1)
        decay = jnp.concatenate(dec_parts, axis=1)
        h_new = h_t * decay + jnp.dot(bg_t, xw, preferred_element_type=F32)
        h_ref[g] = h_new

        @pl.when((last_ref[step] == 1) & (islat_ref[step] == 0))
        def _():
            fin_ref[0, 0, g * GROUP_W:(g + 1) * GROUP_W, :] = h_new.T


def _ssd_call(xbc, dt, a_log, state_ssd, tabs, layer, batch):
    n_tok = xbc.shape[0]
    q = SSM_CHUNK
    n_steps = n_tok // q
    hpn = SSM_HEADS * SSM_HEAD_DIM
    init = state_ssd.reshape(state_ssd.shape[0], state_ssd.shape[1], 2, hpn, SSM_STATE)

    def sidx(d, j):
        return d * n_steps + j

    grid_spec = pltpu.PrefetchScalarGridSpec(
        num_scalar_prefetch=6,
        grid=(2, n_steps),
        in_specs=[
            pl.BlockSpec((q, CONV_CH), lambda d, j, ch, *_: (ch[sidx(d, j)], 0)),
            pl.BlockSpec((q, LANES), lambda d, j, ch, *_: (ch[sidx(d, j)], d)),
            pl.BlockSpec((1, 1, 1, LANES), lambda d, j, *_: (d, 0, 0, 0)),
            pl.BlockSpec((1, 1, 1, hpn, SSM_STATE),
                         lambda d, j, ch, fi, la, il, lb, cb: (lb[sidx(d, j)], layer, d, 0, 0)),
        ],
        out_specs=[
            pl.BlockSpec((1, q, D_INNER), lambda d, j, ch, *_: (d, ch[sidx(d, j)], 0)),
            pl.BlockSpec((1, 1, hpn, SSM_STATE),
                         lambda d, j, ch, fi, la, il, lb, cb: (cb[sidx(d, j)], d, 0, 0)),
        ],
        scratch_shapes=[pltpu.VMEM((SSM_GROUPS, SSM_STATE, GROUP_W), F32)],
    )
    return pl.pallas_call(
        functools.partial(_ssd_kernel, n_steps=n_steps),
        grid_spec=grid_spec,
        out_shape=[jax.ShapeDtypeStruct((2, n_tok, D_INNER), F32),
                   jax.ShapeDtypeStruct((batch, 2, hpn, SSM_STATE), F32)],
        compiler_params=_params(2),
        name="ssd_scan",
    )(tabs["s_chunk"], tabs["s_first"], tabs["s_last"], tabs["s_islat"], tabs["s_latb"], tabs["s_ctxb"],
      xbc, dt, a_log, init)


def _comb_kernel(crow_ref, attn_ref, y_ref, xs_ref, z_ref, ga_ref, gs_ref, x_ref, mod_ref,
                 dskip_ref, nw_ref, wa_ref, ws_ref, wo_ref, x1_ref, h2_ref):
    i = pl.program_id(0)
    mod = mod_ref[pl.ds(crow_ref[i], 1), :]
    gate1 = mod[:, 2 * D_MODEL:3 * D_MODEL]
    shift2 = mod[:, 3 * D_MODEL:4 * D_MODEL]
    scale2 = mod[:, 4 * D_MODEL:5 * D_MODEL]
    y = y_ref[0] + y_ref[1] + dskip_ref[...] * xs_ref[...]
    y = y * _silu(z_ref[...])
    yn = (y * lax.rsqrt(jnp.mean(y * y, axis=-1, keepdims=True) + EPS)) * nw_ref[...]
    merged = (ga_ref[...] * jnp.dot(attn_ref[...], wa_ref[...], preferred_element_type=F32)
              + gs_ref[...] * jnp.dot(yn.astype(BF16), ws_ref[...], preferred_element_type=F32))
    x1 = x_ref[...] + gate1 * jnp.dot(merged.astype(BF16), wo_ref[...], preferred_element_type=F32)
    x1_ref[...] = x1
    h2_ref[...] = _modulate(x1, shift2, scale2).astype(BF16)


def _comb_call(tabs, attn, y, xbc, z, ga, gs, x, mod, d_skip, ssd_norm, w_attn_o, w_ssd_o, w_out):
    n_tok = x.shape[0]
    tm = ROW_TILE

    def row(width):
        return pl.BlockSpec((tm, width), lambda i, *_: (i, 0))

    def const(shape):
        return pl.BlockSpec(shape, lambda i, *_: (0,) * len(shape))

    grid_spec = pltpu.PrefetchScalarGridSpec(
        num_scalar_prefetch=1,
        grid=(n_tok // tm,),
        in_specs=[row(ATTN_W),
                  pl.BlockSpec((2, tm, D_INNER), lambda i, *_: (0, i, 0)),
                  row(D_INNER), row(D_INNER), row(D_MODEL), row(D_MODEL), row(D_MODEL),
                  const(mod.shape), const((1, D_INNER)), const((1, D_INNER)),
                  _resident((ATTN_W, D_MODEL), lambda i, *_: (0, 0)),
                  _resident((D_INNER, D_MODEL), lambda i, *_: (0, 0)),
                  _resident((D_MODEL, D_MODEL), lambda i, *_: (0, 0))],
        out_specs=[row(D_MODEL), row(D_MODEL)],
    )
    return pl.pallas_call(
        _comb_kernel,
        grid_spec=grid_spec,
        out_shape=[jax.ShapeDtypeStruct((n_tok, D_MODEL), F32),
                   jax.ShapeDtypeStruct((n_tok, D_MODEL), BF16)],
        compiler_params=_params(1),
        name="combine",
    )(tabs["crow"], attn, y, xbc, z, ga, gs, x, mod, d_skip, ssd_norm, w_attn_o, w_ssd_o, w_out)


FFN_COL_CHUNK = D_FF // 2


def _ffn_kernel(crow_ref, hprev_ref, hnext_ref,
                h_ref, hp_ref, hn_ref, x_ref, mod_ref, cw_ref, cb_ref, wu_ref, wd_ref,
                o_ref, sv_ref, sg_ref):
    i = pl.program_id(0)
    tm = h_ref.shape[0]
    halo = hp_ref.shape[0]
    mod = mod_ref[pl.ds(crow_ref[i], 1), :]
    gate2 = mod[:, 5 * D_MODEL:6 * D_MODEL]
    h_ext = jnp.concatenate([hp_ref[...], h_ref[...], hn_ref[...]], axis=0)
    acc = jnp.zeros((tm, D_MODEL), F32)
    cc = FFN_COL_CHUNK
    for c in range(D_FF // cc):
        halves = []
        for s_ref, base in ((sv_ref, c * cc), (sg_ref, D_FF + c * cc)):
            s_ref[...] = jnp.dot(h_ext, wu_ref[:, base:base + cc], preferred_element_type=F32)

            @pl.when(hprev_ref[i] == 0)
            def _():
                s_ref[0:halo, :] = jnp.zeros((halo, cc), F32)

            @pl.when(hnext_ref[i] == 0)
            def _():
                s_ref[halo + tm:2 * halo + tm, :] = jnp.zeros((halo, cc), F32)

            u = (cb_ref[:, base:base + cc]
                 + cw_ref[0:1, base:base + cc] * s_ref[halo - 1:halo - 1 + tm, :]
                 + cw_ref[1:2, base:base + cc] * s_ref[halo:halo + tm, :]
                 + cw_ref[2:3, base:base + cc] * s_ref[halo + 1:halo + 1 + tm, :])
            halves.append(u)
        act = (_silu(halves[1]) * halves[0]).astype(BF16)
        acc = acc + jnp.dot(act, wd_ref[c * cc:(c + 1) * cc, :], preferred_element_type=F32)
    o_ref[...] = x_ref[...] + gate2 * acc


def _ffn_call(tabs, h2, x1, mod, ffn_conv_w, ffn_conv_b, w_up, w_down):
    n_tok = x1.shape[0]
    tm = ROW_TILE
    halo = SUBLANES_BF16
    hb = tm // halo
    last_hb = n_tok // halo - 1

    def const(shape):
        return pl.BlockSpec(shape, lambda i, *_: (0,) * len(shape))

    grid_spec = pltpu.PrefetchScalarGridSpec(
        num_scalar_prefetch=3,
        grid=(n_tok // tm,),
        in_specs=[pl.BlockSpec((tm, D_MODEL), lambda i, *_: (i, 0)),
                  pl.BlockSpec((halo, D_MODEL), lambda i, *_: (jnp.maximum(i * hb - 1, 0), 0)),
                  pl.BlockSpec((halo, D_MODEL), lambda i, *_: (jnp.minimum((i + 1) * hb, last_hb), 0)),
                  pl.BlockSpec((tm, D_MODEL), lambda i, *_: (i, 0)),
                  const(mod.shape), const((3, 2 * D_FF)), const((1, 2 * D_FF)),
                  _resident((D_MODEL, 2 * D_FF), lambda i, *_: (0, 0)),
                  _resident((D_FF, D_MODEL), lambda i, *_: (0, 0))],
        out_specs=pl.BlockSpec((tm, D_MODEL), lambda i, *_: (i, 0)),
        scratch_shapes=[pltpu.VMEM((tm + 2 * halo, FFN_COL_CHUNK), F32),
                        pltpu.VMEM((tm + 2 * halo, FFN_COL_CHUNK), F32)],
    )
    return pl.pallas_call(
        _ffn_kernel,
        grid_spec=grid_spec,
        out_shape=jax.ShapeDtypeStruct((n_tok, D_MODEL), F32),
        compiler_params=_params(1),
        name="conv_ffn",
    )(tabs["crow"], tabs["hprev"], tabs["hnext"], h2, h2, h2, x1, mod, ffn_conv_w, ffn_conv_b, w_up, w_down)


def _tables(batch, seq, dec_batch, dec_seq):
    tm = ROW_TILE
    crow, hprev, hnext, rblk = [], [], [], []
    for b in range(batch):
        for t in range(seq // tm):
            crow.append(0)
            hprev.append(int(t > 0))
            hnext.append(int(t < seq // tm - 1))
            rblk.append(0)
    for b in range(dec_batch):
        for t in range(dec_seq // tm):
            crow.append(1 + b)
            hprev.append(int(t > 0))
            hnext.append(int(t < dec_seq // tm - 1))
            rblk.append(1 + t)
    q = SSM_CHUNK
    seqs = [(False, b, seq // q) for b in range(batch)] + [(True, b, dec_seq // q) for b in range(dec_batch)]
    chunks = []
    base = 0
    for is_lat, b, n in seqs:
        for c in range(n):
            chunks.append((base + c, is_lat, b, c == 0, c == n - 1))
        base += n
    s_chunk, s_first, s_last, s_islat, s_latb, s_ctxb = [], [], [], [], [], []
    for d in range(2):
        order = chunks if d == 0 else chunks[::-1]
        for idx, is_lat, b, at_start, at_end in order:
            s_chunk.append(idx)
            s_first.append(int(at_start if d == 0 else at_end))
            s_last.append(int(at_end if d == 0 else at_start))
            s_islat.append(int(is_lat))
            s_latb.append(b if is_lat else 0)
            s_ctxb.append(batch - 1 if is_lat else b)
    tabs = dict(crow=crow, hprev=hprev, hnext=hnext, rblk=rblk, s_chunk=s_chunk, s_first=s_first,
                s_last=s_last, s_islat=s_islat, s_latb=s_latb, s_ctxb=s_ctxb)
    return {k: jnp.asarray(np.asarray(v, np.int32)) for k, v in tabs.items()}


def _rope_tables(dec_seq):
    rows = dec_seq // GRID_W
    row = jnp.repeat(jnp.arange(rows), GRID_W).astype(F32)
    col = jnp.tile(jnp.arange(GRID_W), rows).astype(F32)
    half = HEAD_DIM // 2
    inv_freq = ROPE_THETA ** (-jnp.arange(0, half, 2, dtype=F32) / half)
    ang_r = row[:, None] * inv_freq
    ang_c = col[:, None] * inv_freq
    ang = jnp.concatenate([ang_r, ang_r, ang_c, ang_c], axis=-1)
    sign = jnp.where((jnp.arange(HEAD_DIM) & (HEAD_DIM // 4)) == 0, -1.0, 1.0).astype(F32)
    cos = jnp.concatenate([jnp.ones((ROW_TILE, HEAD_DIM), F32), jnp.cos(ang)], axis=0)
    sin = jnp.concatenate([jnp.zeros((ROW_TILE, HEAD_DIM), F32), jnp.sin(ang) * sign], axis=0)
    return cos, sin


def _pack_w_in(w_in):
    depth = w_in.shape[0]
    c_dt = ATTN_W + 2 * KV_W + D_INNER + CONV_CH
    pad = jnp.zeros((depth, D_MODEL, LANES - SSM_HEADS), w_in.dtype)
    return jnp.concatenate([w_in[:, :, :c_dt], w_in[:, :, c_dt:c_dt + SSM_HEADS], pad,
                            w_in[:, :, c_dt + SSM_HEADS:c_dt + 2 * SSM_HEADS], pad,
                            w_in[:, :, c_dt + 2 * SSM_HEADS:]], axis=-1).astype(BF16)


def _pad_dir(p):
    return jnp.pad(p, ((0, 0), (0, 0), (0, LANES - SSM_HEADS)))


def kernel(x_prompt, x_sample, c, cache_k, cache_v, state_ssd, c_ctx, w_mod, b_mod, w_in, q_norm, k_norm,
           conv_w, conv_b, dt_bias, a_log, d_skip, ssd_norm, w_attn_o, w_ssd_o, w_out, w_up,
           ffn_conv_w, ffn_conv_b, w_down):
    batch, seq, _ = x_prompt.shape
    dec_batch, dec_seq, _ = x_sample.shape
    depth = w_in.shape[0]
    past = cache_k.shape[2]
    n_ctx = batch * seq
    assert seq % ROW_TILE == 0 and dec_seq % ROW_TILE == 0 and n_ctx % dec_seq == 0
    assert 1 + dec_batch <= SUBLANES_F32

    tabs = _tables(batch, seq, dec_batch, dec_seq)
    rope_cos, rope_sin = _rope_tables(dec_seq)
    cond = jnp.concatenate([c_ctx[None, :], c, jnp.zeros((SUBLANES_F32 - 1 - dec_batch, D_MODEL), F32)], axis=0)
    mods = _mod_call(cond, w_mod, b_mod)

    w_pack = _pack_w_in(w_in)
    wa = w_attn_o.astype(BF16)
    ws = w_ssd_o.astype(BF16)
    wo = w_out.astype(BF16)
    wu = w_up.astype(BF16)
    wd = w_down.astype(BF16)
    dtb = _pad_dir(dt_bias).reshape(depth, 1, 2 * LANES)
    alog = _pad_dir(a_log).reshape(depth, 2, 1, 1, LANES)
    dsk = jnp.repeat(d_skip, SSM_HEAD_DIM, axis=-1).reshape(depth, 1, D_INNER)
    ck = cache_k.reshape(dec_batch, depth, past, KV_W)
    cv = cache_v.reshape(dec_batch, depth, past, KV_W)

    x = jnp.concatenate([x_prompt.reshape(n_ctx, D_MODEL), x_sample.reshape(dec_batch * dec_seq, D_MODEL)], axis=0)
    new_k, new_v, new_s = [], [], []
    for l in range(depth):
        mod = mods[l]
        q, k, v, kf, vf, z, xbc, dt, ga, gs = _in_call(
            x, mod, tabs, rope_cos, rope_sin, q_norm[l][None, :], k_norm[l][None, :], conv_w[l],
            conv_b[l][None, :], dtb[l], w_pack[l])
        attn = _attn_ctx_call(q, k, v, batch, seq)
        attn = _attn_lat_call(q, k, v, ck, cv, attn, n_ctx, dec_batch, dec_seq, l)
        y, fin = _ssd_call(xbc, dt, alog[l], state_ssd, tabs, l, batch)
        x1, h2 = _comb_call(tabs, attn, y, xbc, z, ga, gs, x, mod, dsk[l], ssd_norm[l][None, :],
                            wa[l], ws[l], wo[l])
        x = _ffn_call(tabs, h2, x1, mod, ffn_conv_w[l], ffn_conv_b[l][None, :], wu[l], wd[l])
        new_k.append(kf[:n_ctx].reshape(batch, seq, H_KV, HEAD_DIM))
        new_v.append(vf[:n_ctx].reshape(batch, seq, H_KV, HEAD_DIM))
        new_s.append(fin.reshape(batch, 2, SSM_HEADS, SSM_HEAD_DIM, SSM_STATE))
    y_prompt = x[:n_ctx].reshape(batch, seq, D_MODEL)
    y_sample = x[n_ctx:].reshape(dec_batch, dec_seq, D_MODEL)
    return (y_prompt, y_sample, jnp.stack(new_k, axis=1), jnp.stack(new_v, axis=1), jnp.stack(new_s, axis=1))
```

```python
import functools
import math

import numpy as np
import jax
import jax.numpy as jnp
from jax import lax
from jax.experimental import pallas as pl
from jax.experimental.pallas import tpu as pltpu

F32 = jnp.float32
BF16 = jnp.bfloat16

D_MODEL = 1024
GRID_W = 64
EPS = 1e-6
H_Q = 8
H_KV = 2
HEAD_DIM = 128
ATTN_W = H_Q * HEAD_DIM
KV_W = H_KV * HEAD_DIM
ROPE_THETA = 10000.0
D_INNER = 2 * D_MODEL
SSM_HEAD_DIM = 64
SSM_HEADS = D_INNER // SSM_HEAD_DIM
SSM_GROUPS = 4
SSM_STATE = 128
SSM_CHUNK = 128
CONV_CH = D_INNER + 2 * SSM_GROUPS * SSM_STATE
D_FF = 2816
GROUP_W = (SSM_HEADS // SSM_GROUPS) * SSM_HEAD_DIM
LOG2E = 1.4426950408889634

LANES = 128
SUBLANES_F32 = 8
SUBLANES_BF16 = 16
VMEM_LIMIT_BYTES = 56 * 1024 * 1024

Q0 = 0
K0 = Q0 + ATTN_W
V0 = K0 + KV_W
Z0 = V0 + KV_W
X0 = Z0 + D_INNER
DT0 = X0 + CONV_CH
GA0 = DT0 + 2 * LANES
GS0 = GA0 + D_MODEL
N_PACK = GS0 + D_MODEL
DT_COPIES = 3

ROW_TILE = 256


def _sigmoid(x):
    return 1.0 / (1.0 + jnp.exp(-x))


def _silu(x):
    return x * _sigmoid(x)


def _params(n_axes):
    return pltpu.CompilerParams(dimension_semantics=("arbitrary",) * n_axes,
                                vmem_limit_bytes=VMEM_LIMIT_BYTES)


def _resident(shape, index_map):
    return pl.BlockSpec(shape, index_map, pipeline_mode=pl.Buffered(1))


def _layer_block(arr, layer, resident=False):
    shape = (None,) + tuple(arr.shape[1:])
    nz = len(arr.shape) - 1

    def index_map(*_):
        return (layer,) + (0,) * nz

    return _resident(shape, index_map) if resident else pl.BlockSpec(shape, index_map)


def _mod_kernel(cond_ref, w_ref, b_ref, o_ref):
    a = _silu(cond_ref[...]).astype(BF16)
    o_ref[0] = jnp.dot(a, w_ref[0].astype(BF16), preferred_element_type=F32) + b_ref[0]


def _mod_call(cond, w_mod, b_mod):
    depth, _, n = w_mod.shape
    tn = 1536
    return pl.pallas_call(
        _mod_kernel,
        grid=(depth, n // tn),
        in_specs=[pl.BlockSpec(cond.shape, lambda l, j: (0, 0)),
                  pl.BlockSpec((1, D_MODEL, tn), lambda l, j: (l, 0, j)),
                  pl.BlockSpec((1, 1, tn), lambda l, j: (l, 0, j))],
        out_specs=pl.BlockSpec((1, cond.shape[0], tn), lambda l, j: (l, 0, j)),
        out_shape=jax.ShapeDtypeStruct((depth, cond.shape[0], n), F32),
        compiler_params=_params(2),
        name="mod_vectors",
    )(cond, w_mod, b_mod.reshape(depth, 1, n))


def _modulate(x, shift, scale):
    ms = jnp.mean(x * x, axis=-1, keepdims=True)
    return (x * lax.rsqrt(ms + EPS)) * (1.0 + scale) + shift


def _in_kernel(crow_ref, hprev_ref, hnext_ref, rblk_ref,
               x_ref, xp_ref, xn_ref, mod_ref, cos_ref, sin_ref, qn_ref, kn_ref, cw_ref, cb_ref,
               dtb_ref, w_ref,
               q_ref, k_ref, v_ref, kf_ref, vf_ref, z_ref, xbc_ref, dt_ref, ga_ref, gs_ref,
               s_ref):
    del rblk_ref
    i = pl.program_id(0)
    tm = x_ref.shape[0]
    halo = xp_ref.shape[0]
    mod = mod_ref[pl.ds(crow_ref[i], 1), :]
    shift = mod[:, 0:D_MODEL]
    scale = mod[:, D_MODEL:2 * D_MODEL]
    hm = _modulate(x_ref[...], shift, scale)
    hp = _modulate(xp_ref[...], shift, scale) * hprev_ref[i].astype(F32)
    hn = _modulate(xn_ref[...], shift, scale) * hnext_ref[i].astype(F32)
    h = hm.astype(BF16)
    h_ext = jnp.concatenate([hp, hm, hn], axis=0).astype(BF16)

    qkv = jnp.dot(h, w_ref[:, Q0:Z0], preferred_element_type=F32)
    cos = cos_ref[...]
    sin = sin_ref[...]
    lane = lax.broadcasted_iota(jnp.int32, (tm, HEAD_DIM), 1)
    first_quarter = (lane & (HEAD_DIM // 4)) == 0

    def norm_rope(u, w):
        un = u * lax.rsqrt(jnp.mean(u * u, axis=-1, keepdims=True) + EPS) * w
        rot = jnp.where(first_quarter, pltpu.roll(un, HEAD_DIM - HEAD_DIM // 4, 1),
                        pltpu.roll(un, HEAD_DIM // 4, 1))
        return un * cos + rot * sin

    q_scale = HEAD_DIM ** -0.5 * LOG2E
    for j in range(H_Q):
        qh = norm_rope(qkv[:, j * HEAD_DIM:(j + 1) * HEAD_DIM], qn_ref[...])
        q_ref[:, j * HEAD_DIM:(j + 1) * HEAD_DIM] = (qh * q_scale).astype(BF16)
    ones = jnp.ones((tm, HEAD_DIM), BF16)
    for j in range(H_KV):
        kh = norm_rope(qkv[:, K0 + j * HEAD_DIM:K0 + (j + 1) * HEAD_DIM], kn_ref[...])
        k_ref[:, j * HEAD_DIM:(j + 1) * HEAD_DIM] = kh.astype(BF16)
        kf_ref[:, j * HEAD_DIM:(j + 1) * HEAD_DIM] = kh
        v_ref[:, 2 * j * HEAD_DIM:(2 * j + 1) * HEAD_DIM] = qkv[:, V0 + j * HEAD_DIM:V0 + (j + 1) * HEAD_DIM].astype(BF16)
        v_ref[:, (2 * j + 1) * HEAD_DIM:(2 * j + 2) * HEAD_DIM] = ones
    vf_ref[...] = qkv[:, V0:Z0]

    z_ref[...] = _silu(jnp.dot(h, w_ref[:, Z0:X0], preferred_element_type=F32)).astype(BF16)

    s_ref[...] = jnp.dot(h_ext, w_ref[:, X0:DT0], preferred_element_type=F32)
    xc = (cb_ref[...] + cw_ref[0:1, :] * s_ref[halo - 1:halo - 1 + tm, :]
          + cw_ref[1:2, :] * s_ref[halo:halo + tm, :]
          + cw_ref[2:3, :] * s_ref[halo + 1:halo + 1 + tm, :])
    xbc_ref[...] = _silu(xc).astype(BF16)

    dt_raw = jnp.dot(h, w_ref[:, DT0:GA0], preferred_element_type=F32) + dtb_ref[...]
    dt_ref[...] = jnp.maximum(dt_raw, 0.0) + jnp.log1p(jnp.exp(-jnp.abs(dt_raw)))

    ga_ref[...] = _sigmoid(jnp.dot(h, w_ref[:, GA0:GS0], preferred_element_type=F32)).astype(BF16)
    gs_ref[...] = _sigmoid(jnp.dot(h, w_ref[:, GS0:N_PACK], preferred_element_type=F32)).astype(BF16)


def _in_call(layer, x, mods, tabs, rope_cos, rope_sin, q_norm, k_norm, conv_w, conv_b, dt_bias, w_pack):
    n_tok = x.shape[0]
    tm = ROW_TILE
    halo = SUBLANES_F32
    n_tiles = n_tok // tm
    hb = tm // halo
    last_hb = n_tok // halo - 1

    def row(width):
        return pl.BlockSpec((tm, width), lambda i, *_: (i, 0))

    grid_spec = pltpu.PrefetchScalarGridSpec(
        num_scalar_prefetch=4,
        grid=(n_tiles,),
        in_specs=[
            row(D_MODEL),
            pl.BlockSpec((halo, D_MODEL), lambda i, *_: (jnp.maximum(i * hb - 1, 0), 0)),
            pl.BlockSpec((halo, D_MODEL), lambda i, *_: (jnp.minimum((i + 1) * hb, last_hb), 0)),
            _layer_block(mods, layer),
            pl.BlockSpec((tm, HEAD_DIM), lambda i, c, p, n, r: (r[i], 0)),
            pl.BlockSpec((tm, HEAD_DIM), lambda i, c, p, n, r: (r[i], 0)),
            _layer_block(q_norm, layer), _layer_block(k_norm, layer),
            _layer_block(conv_w, layer), _layer_block(conv_b, layer), _layer_block(dt_bias, layer),
            _layer_block(w_pack, layer, resident=True),
        ],
        out_specs=[row(ATTN_W), row(KV_W), row(2 * KV_W), row(KV_W), row(KV_W), row(D_INNER),
                   row(CONV_CH), row(2 * LANES), row(D_MODEL), row(D_MODEL)],
        scratch_shapes=[pltpu.VMEM((tm + 2 * halo, CONV_CH), F32)],
    )
    sds = jax.ShapeDtypeStruct
    return pl.pallas_call(
        _in_kernel,
        grid_spec=grid_spec,
        out_shape=[sds((n_tok, ATTN_W), BF16), sds((n_tok, KV_W), BF16), sds((n_tok, 2 * KV_W), BF16),
                   sds((n_tok, KV_W), F32), sds((n_tok, KV_W), F32), sds((n_tok, D_INNER), BF16),
                   sds((n_tok, CONV_CH), BF16), sds((n_tok, 2 * LANES), F32),
                   sds((n_tok, D_MODEL), BF16), sds((n_tok, D_MODEL), BF16)],
        compiler_params=_params(1),
        name="in_proj",
    )(tabs["crow"], tabs["hprev"], tabs["hnext"], tabs["rblk"],
      x, x, x, mods, rope_cos, rope_sin, q_norm, k_norm, conv_w, conv_b, dt_bias, w_pack)


def _attn_kernel(*refs, has_cache):
    if has_cache:
        q_ref, k_ref, v_ref, kc_ref, vc_ref, _, o_ref, kt_ref, vcx_ref = refs
        past = kc_ref.shape[0]
    else:
        q_ref, k_ref, v_ref, o_ref, kt_ref = refs
        past = 0
    rep = H_Q // H_KV

    @pl.when(pl.program_id(1) == 0)
    def _():
        if has_cache:
            kt_ref[:, 0:past] = kc_ref[...].T.astype(BF16)
            for g in range(H_KV):
                vcx_ref[:, 2 * g * HEAD_DIM:(2 * g + 1) * HEAD_DIM] = vc_ref[:, g * HEAD_DIM:(g + 1) * HEAD_DIM].astype(BF16)
                vcx_ref[:, (2 * g + 1) * HEAD_DIM:(2 * g + 2) * HEAD_DIM] = jnp.ones((past, HEAD_DIM), BF16)
        kt_ref[:, past:] = k_ref[...].astype(F32).T.astype(BF16)

    for j in range(H_Q):
        g = j // rep
        q = q_ref[:, j * HEAD_DIM:(j + 1) * HEAD_DIM]
        s = jnp.dot(q, kt_ref[g * HEAD_DIM:(g + 1) * HEAD_DIM, :], preferred_element_type=F32)
        p = jnp.exp2(s - jnp.max(s, axis=-1, keepdims=True)).astype(BF16)
        o = jnp.dot(p[:, past:], v_ref[:, 2 * g * HEAD_DIM:(2 * g + 2) * HEAD_DIM], preferred_element_type=F32)
        if has_cache:
            o = o + jnp.dot(p[:, 0:past], vcx_ref[:, 2 * g * HEAD_DIM:(2 * g + 2) * HEAD_DIM],
                            preferred_element_type=F32)
        o_ref[:, j * HEAD_DIM:(j + 1) * HEAD_DIM] = (o[:, 0:HEAD_DIM] / o[:, HEAD_DIM:2 * HEAD_DIM]).astype(BF16)


def _attn_ctx_call(q, k, v, batch, seq):
    n_tok = q.shape[0]
    return pl.pallas_call(
        functools.partial(_attn_kernel, has_cache=False),
        grid=(batch, 1),
        in_specs=[pl.BlockSpec((seq, ATTN_W), lambda b, t: (b, 0)),
                  pl.BlockSpec((seq, KV_W), lambda b, t: (b, 0)),
                  pl.BlockSpec((seq, 2 * KV_W), lambda b, t: (b, 0))],
        out_specs=pl.BlockSpec((seq, ATTN_W), lambda b, t: (b, 0)),
        out_shape=jax.ShapeDtypeStruct((n_tok, ATTN_W), BF16),
        scratch_shapes=[pltpu.VMEM((KV_W, seq), BF16)],
        compiler_params=_params(2),
        name="attn_context",
    )(q, k, v)


def _attn_lat_call(q, k, v, cache_k, cache_v, attn, n_ctx_tok, dec_batch, dec_seq, layer):
    tq = ROW_TILE
    t0 = n_ctx_tok // tq
    s0 = n_ctx_tok // dec_seq
    tiles = dec_seq // tq
    past = cache_k.shape[2]
    return pl.pallas_call(
        functools.partial(_attn_kernel, has_cache=True),
        grid=(dec_batch, tiles),
        in_specs=[pl.BlockSpec((tq, ATTN_W), lambda b, t: (t0 + b * tiles + t, 0)),
                  pl.BlockSpec((dec_seq, KV_W), lambda b, t: (s0 + b, 0)),
                  pl.BlockSpec((dec_seq, 2 * KV_W), lambda b, t: (s0 + b, 0)),
                  pl.BlockSpec((None, None, past, KV_W), lambda b, t: (b, layer, 0, 0)),
                  pl.BlockSpec((None, None, past, KV_W), lambda b, t: (b, layer, 0, 0)),
                  pl.BlockSpec(memory_space=pl.ANY)],
        out_specs=pl.BlockSpec((tq, ATTN_W), lambda b, t: (t0 + b * tiles + t, 0)),
        out_shape=jax.ShapeDtypeStruct(attn.shape, attn.dtype),
        scratch_shapes=[pltpu.VMEM((KV_W, past + dec_seq), BF16),
                        pltpu.VMEM((past, 2 * KV_W), BF16)],
        input_output_aliases={5: 0},
        compiler_params=_params(2),
        name="attn_latent",
    )(q, k, v, cache_k, cache_v, attn)


def _bf16_split3(x):
    c1 = x.astype(BF16).astype(F32)
    r = x - c1
    c2 = r.astype(BF16).astype(F32)
    c3 = (r - c2).astype(BF16).astype(F32)
    return c1, c2, c3


def _ssd_direction(x_ref, dt_ref, alog, h_ref, y_ref, ecol_ref, ew_ref, d):
    q = SSM_CHUNK
    heads_per_group = SSM_HEADS // SSM_GROUPS
    dt = dt_ref[...]
    a = -jnp.exp(alog)
    row = lax.broadcasted_iota(jnp.int32, (q, q), 0)
    col = lax.broadcasted_iota(jnp.int32, (q, q), 1)
    mask = (row >= col) if d == 0 else (row <= col)
    ones_tri = jnp.where(mask, 1.0, 0.0).astype(BF16)
    lane = lax.broadcasted_iota(jnp.int32, (q, LANES), 1)

    def pick_split(parts):
        return jnp.where(lane < SSM_HEADS, parts[0],
                         jnp.where(lane < 2 * SSM_HEADS, parts[1],
                                   jnp.where(lane < 3 * SSM_HEADS, parts[2], 0.0))).astype(BF16)

    da = jnp.concatenate(_bf16_split3(dt * a), axis=1).astype(BF16)
    cs = jnp.dot(ones_tri, da, preferred_element_type=F32)
    cum = (cs[:, 0:LANES] + cs[:, LANES:2 * LANES] + cs[:, 2 * LANES:3 * LANES]) * LOG2E
    cum_t = cum.T
    dt_t = dt.T
    total = cum[q - 1:q, :] if d == 0 else cum[0:1, :]
    state_w = dt * jnp.exp2(total - cum)
    col_all = jnp.dot(pick_split(_bf16_split3(cum)), ecol_ref[...], preferred_element_type=F32)
    sw_all = jnp.dot(pick_split(_bf16_split3(state_w)), ew_ref[...], preferred_element_type=F32)
    lo = lane < SSM_HEAD_DIM
    lo_row = lo[0:1, :]
    last = q - 1 if d == 0 else 0
    row2 = lax.broadcasted_iota(jnp.int32, (2 * q, LANES), 0)
    lane2 = lax.broadcasted_iota(jnp.int32, (2 * q, LANES), 1)
    pair_mask = jnp.where((row2 < q) == (lane2 < SSM_HEAD_DIM), 1.0, 0.0).astype(BF16)

    for g in range(SSM_GROUPS):
        b0 = D_INNER + g * SSM_STATE
        c0 = D_INNER + SSM_GROUPS * SSM_STATE + g * SSM_STATE
        bg_t = x_ref[:, b0:b0 + SSM_STATE].astype(F32).T.astype(BF16)
        cg = x_ref[:, c0:c0 + SSM_STATE]
        cb = jnp.dot(cg, bg_t, preferred_element_type=F32)
        h_t = h_ref[d, g]
        y_off = jnp.dot(cg, h_t.astype(BF16), preferred_element_type=F32)
        xw_parts = []
        dec_parts = []
        for pp in range(heads_per_group // 2):
            h0 = g * heads_per_group + 2 * pp
            h1 = h0 + 1
            col0 = col_all[:, h0 * LANES:(h0 + 1) * LANES]
            col1 = col_all[:, h1 * LANES:(h1 + 1) * LANES]
            w0 = cb * jnp.exp2(jnp.where(mask, col0 - cum_t[h0:h0 + 1, :], -jnp.inf)) * dt_t[h0:h0 + 1, :]
            w1 = cb * jnp.exp2(jnp.where(mask, col1 - cum_t[h1:h1 + 1, :], -jnp.inf)) * dt_t[h1:h1 + 1, :]
            x0 = h0 * SSM_HEAD_DIM
            xpair = x_ref[:, x0:x0 + LANES]
            w01 = jnp.concatenate([w0.astype(BF16), w1.astype(BF16)], axis=1)
            x01 = jnp.concatenate([xpair, xpair], axis=0) * pair_mask
            y_diag = jnp.dot(w01, x01, preferred_element_type=F32)
            colsel = jnp.where(lo, col0, col1)
            y_pair = y_diag + y_off[:, pp * LANES:(pp + 1) * LANES] * jnp.exp2(colsel)
            y_ref[:, x0:x0 + LANES] = y_pair.astype(BF16)
            xw_parts.append((xpair.astype(F32) * sw_all[:, x0:x0 + LANES]).astype(BF16))
            dec_parts.append(jnp.exp2(jnp.where(lo_row, col0[last:last + 1, :], col1[last:last + 1, :])))
        xw = jnp.concatenate(xw_parts, axis=1)
        decay = jnp.concatenate(dec_parts, axis=1)
        h_ref[d, g] = h_t * decay + jnp.dot(bg_t, xw, preferred_element_type=F32)


def _ssd_kernel(chf_ref, chb_ref, first_ref, last_ref, islat_ref, latb_ref, ctxb_ref,
                xf_ref, xb_ref, dtf_ref, dtb_ref, alog_ref, init_ref, ecol_ref, ew_ref,
                yf_ref, yb_ref, fin_ref, h_ref):
    del chf_ref, chb_ref, latb_ref, ctxb_ref
    step = pl.program_id(0)

    @pl.when((first_ref[step] == 1) & (islat_ref[step] == 0))
    def _():
        h_ref[...] = jnp.zeros(h_ref.shape, F32)

    @pl.when((first_ref[step] == 1) & (islat_ref[step] == 1))
    def _():
        for d in range(2):
            for g in range(SSM_GROUPS):
                h_ref[d, g] = init_ref[d, g * GROUP_W:(g + 1) * GROUP_W, :].T

    _ssd_direction(xf_ref, dtf_ref, alog_ref[0], h_ref, yf_ref, ecol_ref, ew_ref, 0)
    _ssd_direction(xb_ref, dtb_ref, alog_ref[1], h_ref, yb_ref, ecol_ref, ew_ref, 1)

    @pl.when((last_ref[step] == 1) & (islat_ref[step] == 0))
    def _():
        for d in range(2):
            for g in range(SSM_GROUPS):
                fin_ref[d, g * GROUP_W:(g + 1) * GROUP_W, :] = h_ref[d, g].T


def _expansion_matrices():
    r = np.arange(LANES)
    head = r % SSM_HEADS
    live = r < DT_COPIES * SSM_HEADS
    ecol = (live[:, None] & (head[:, None] == (np.arange(SSM_HEADS * LANES) // LANES)[None, :]))
    ew = (live[:, None] & (head[:, None] == (np.arange(D_INNER) // SSM_HEAD_DIM)[None, :]))
    return jnp.asarray(ecol, BF16), jnp.asarray(ew, BF16)


def _ssd_call(layer, xbc, dt, a_log, state_ssd, tabs, new_s):
    n_tok = xbc.shape[0]
    q = SSM_CHUNK
    n_steps = n_tok // q
    hpn = SSM_HEADS * SSM_HEAD_DIM
    dec_batch, depth = state_ssd.shape[0], state_ssd.shape[1]
    batch = int(tabs["batch"])
    init = state_ssd.reshape(dec_batch, depth, 2, hpn, SSM_STATE)
    ecol, ew = _expansion_matrices()

    in_specs = [
        pl.BlockSpec((q, CONV_CH), lambda j, cf, cb, *_: (cf[j], 0)),
        pl.BlockSpec((q, CONV_CH), lambda j, cf, cb, *_: (cb[j], 0)),
        pl.BlockSpec((q, LANES), lambda j, cf, cb, *_: (cf[j], 0)),
        pl.BlockSpec((q, LANES), lambda j, cf, cb, *_: (cb[j], 1)),
        _layer_block(a_log, layer),
        pl.BlockSpec((None, None, 2, hpn, SSM_STATE),
                     lambda j, cf, cb, fi, la, il, lb, xb: (lb[j], layer, 0, 0, 0)),
        _resident(ecol.shape, lambda j, *_: (0, 0)),
        _resident(ew.shape, lambda j, *_: (0, 0)),
    ]
    args = [xbc, xbc, dt, dt, a_log, init, ecol, ew]
    aliases = {}
    if new_s is not None:
        in_specs.append(pl.BlockSpec(memory_space=pl.ANY))
        args.append(new_s)
        aliases = {7 + len(args) - 1: 2}
    grid_spec = pltpu.PrefetchScalarGridSpec(
        num_scalar_prefetch=7,
        grid=(n_steps,),
        in_specs=in_specs,
        out_specs=[
            pl.BlockSpec((q, D_INNER), lambda j, cf, cb, *_: (cf[j], 0)),
            pl.BlockSpec((q, D_INNER), lambda j, cf, cb, *_: (cb[j], 0)),
            pl.BlockSpec((None, None, 2, hpn, SSM_STATE),
                         lambda j, cf, cb, fi, la, il, lb, xb: (xb[j], layer, 0, 0, 0)),
        ],
        scratch_shapes=[pltpu.VMEM((2, SSM_GROUPS, SSM_STATE, GROUP_W), F32)],
    )

    def body(*refs):
        if new_s is not None:
            refs = refs[:15] + refs[16:]
        _ssd_kernel(*refs)

    return pl.pallas_call(
        body,
        grid_spec=grid_spec,
        out_shape=[jax.ShapeDtypeStruct((n_tok, D_INNER), BF16),
                   jax.ShapeDtypeStruct((n_tok, D_INNER), BF16),
                   jax.ShapeDtypeStruct((batch, depth, 2, hpn, SSM_STATE), F32)],
        input_output_aliases=aliases,
        compiler_params=_params(1),
        name="ssd_scan",
    )(tabs["s_chunk_f"], tabs["s_chunk_b"], tabs["s_first"], tabs["s_last"], tabs["s_islat"],
      tabs["s_latb"], tabs["s_ctxb"], *args)


def _comb_kernel(crow_ref, attn_ref, yf_ref, yb_ref, xs_ref, z_ref, ga_ref, gs_ref, x_ref, mod_ref,
                 dskip_ref, nw_ref, wa_ref, ws_ref, wo_ref, x1_ref, h2_ref):
    i = pl.program_id(0)
    mod = mod_ref[pl.ds(crow_ref[i], 1), :]
    gate1 = mod[:, 2 * D_MODEL:3 * D_MODEL]
    shift2 = mod[:, 3 * D_MODEL:4 * D_MODEL]
    scale2 = mod[:, 4 * D_MODEL:5 * D_MODEL]
    y = yf_ref[...].astype(F32) + yb_ref[...].astype(F32) + dskip_ref[...] * xs_ref[...].astype(F32)
    y = y * z_ref[...].astype(F32)
    yn = (y * lax.rsqrt(jnp.mean(y * y, axis=-1, keepdims=True) + EPS)) * nw_ref[...]
    merged = (ga_ref[...].astype(F32) * jnp.dot(attn_ref[...], wa_ref[...], preferred_element_type=F32)
              + gs_ref[...].astype(F32) * jnp.dot(yn.astype(BF16), ws_ref[...], preferred_element_type=F32))
    x1 = x_ref[...] + gate1 * jnp.dot(merged.astype(BF16), wo_ref[...], preferred_element_type=F32)
    x1_ref[...] = x1
    h2_ref[...] = _modulate(x1, shift2, scale2).astype(BF16)


def _comb_call(layer, tabs, attn, yf, yb, xbc, z, ga, gs, x, mods, d_skip, ssd_norm, w_attn_o, w_ssd_o, w_out):
    n_tok = x.shape[0]
    tm = ROW_TILE

    def row(width):
        return pl.BlockSpec((tm, width), lambda i, *_: (i, 0))

    grid_spec = pltpu.PrefetchScalarGridSpec(
        num_scalar_prefetch=1,
        grid=(n_tok // tm,),
        in_specs=[row(ATTN_W), row(D_INNER), row(D_INNER), row(D_INNER), row(D_INNER),
                  row(D_MODEL), row(D_MODEL), row(D_MODEL),
                  _layer_block(mods, layer), _layer_block(d_skip, layer), _layer_block(ssd_norm, layer),
                  _layer_block(w_attn_o, layer, resident=True),
                  _layer_block(w_ssd_o, layer, resident=True),
                  _layer_block(w_out, layer, resident=True)],
        out_specs=[row(D_MODEL), row(D_MODEL)],
    )
    return pl.pallas_call(
        _comb_kernel,
        grid_spec=grid_spec,
        out_shape=[jax.ShapeDtypeStruct((n_tok, D_MODEL), F32),
                   jax.ShapeDtypeStruct((n_tok, D_MODEL), BF16)],
        compiler_params=_params(1),
        name="combine",
    )(tabs["crow"], attn, yf, yb, xbc, z, ga, gs, x, mods, d_skip, ssd_norm, w_attn_o, w_ssd_o, w_out)


FFN_COL_CHUNK = D_FF // 2


def _ffn_kernel(crow_ref, hprev_ref, hnext_ref,
                h_ref, hp_ref, hn_ref, x_ref, mod_ref, cw_ref, cb_ref, wu_ref, wd_ref,
                o_ref, sv_ref, sg_ref):
    i = pl.program_id(0)
    tm = h_ref.shape[0]
    halo = hp_ref.shape[0]
    mod = mod_ref[pl.ds(crow_ref[i], 1), :]
    gate2 = mod[:, 5 * D_MODEL:6 * D_MODEL]
    hp = (hp_ref[...].astype(F32) * hprev_ref[i].astype(F32)).astype(BF16)
    hn = (hn_ref[...].astype(F32) * hnext_ref[i].astype(F32)).astype(BF16)
    h_ext = jnp.concatenate([hp, h_ref[...], hn], axis=0)
    acc = jnp.zeros((tm, D_MODEL), F32)
    cc = FFN_COL_CHUNK
    for c in range(D_FF // cc):
        halves = []
        for s_ref, base in ((sv_ref, c * cc), (sg_ref, D_FF + c * cc)):
            s_ref[...] = jnp.dot(h_ext, wu_ref[:, base:base + cc], preferred_element_type=F32)
            u = (cb_ref[:, base:base + cc]
                 + cw_ref[0:1, base:base + cc] * s_ref[halo - 1:halo - 1 + tm, :]
                 + cw_ref[1:2, base:base + cc] * s_ref[halo:halo + tm, :]
                 + cw_ref[2:3, base:base + cc] * s_ref[halo + 1:halo + 1 + tm, :])
            halves.append(u)
        act = (_silu(halves[1]) * halves[0]).astype(BF16)
        acc = acc + jnp.dot(act, wd_ref[c * cc:(c + 1) * cc, :], preferred_element_type=F32)
    o_ref[...] = x_ref[...] + gate2 * acc


def _ffn_call(layer, tabs, h2, x1, mods, ffn_conv_w, ffn_conv_b, w_up, w_down):
    n_tok = x1.shape[0]
    tm = ROW_TILE
    halo = SUBLANES_BF16
    hb = tm // halo
    last_hb = n_tok // halo - 1

    grid_spec = pltpu.PrefetchScalarGridSpec(
        num_scalar_prefetch=3,
        grid=(n_tok // tm,),
        in_specs=[pl.BlockSpec((tm, D_MODEL), lambda i, *_: (i, 0)),
                  pl.BlockSpec((halo, D_MODEL), lambda i, *_: (jnp.maximum(i * hb - 1, 0), 0)),
                  pl.BlockSpec((halo, D_MODEL), lambda i, *_: (jnp.minimum((i + 1) * hb, last_hb), 0)),
                  pl.BlockSpec((tm, D_MODEL), lambda i, *_: (i, 0)),
                  _layer_block(mods, layer), _layer_block(ffn_conv_w, layer), _layer_block(ffn_conv_b, layer),
                  _layer_block(w_up, layer, resident=True),
                  _layer_block(w_down, layer, resident=True)],
        out_specs=pl.BlockSpec((tm, D_MODEL), lambda i, *_: (i, 0)),
        scratch_shapes=[pltpu.VMEM((tm + 2 * halo, FFN_COL_CHUNK), F32),
                        pltpu.VMEM((tm + 2 * halo, FFN_COL_CHUNK), F32)],
    )
    return pl.pallas_call(
        _ffn_kernel,
        grid_spec=grid_spec,
        out_shape=jax.ShapeDtypeStruct((n_tok, D_MODEL), F32),
        compiler_params=_params(1),
        name="conv_ffn",
    )(tabs["crow"], tabs["hprev"], tabs["hnext"], h2, h2, h2, x1, mods, ffn_conv_w, ffn_conv_b, w_up, w_down)


def _tables(batch, seq, dec_batch, dec_seq):
    tm = ROW_TILE
    crow, hprev, hnext, rblk = [], [], [], []
    for b in range(batch):
        for t in range(seq // tm):
            crow.append(0)
            hprev.append(int(t > 0))
            hnext.append(int(t < seq // tm - 1))
            rblk.append(0)
    for b in range(dec_batch):
        for t in range(dec_seq // tm):
            crow.append(1 + b)
            hprev.append(int(t > 0))
            hnext.append(int(t < dec_seq // tm - 1))
            rblk.append(1 + t)
    q = SSM_CHUNK
    seqs = [(False, b, seq // q) for b in range(batch)] + [(True, b, dec_seq // q) for b in range(dec_batch)]
    s_chunk_f, s_chunk_b, s_first, s_last, s_islat, s_latb, s_ctxb = [], [], [], [], [], [], []
    base = 0
    for is_lat, b, n in seqs:
        for c in range(n):
            s_chunk_f.append(base + c)
            s_chunk_b.append(base + n - 1 - c)
            s_first.append(int(c == 0))
            s_last.append(int(c == n - 1))
            s_islat.append(int(is_lat))
            s_latb.append(b if is_lat else 0)
            s_ctxb.append(batch - 1 if is_lat else b)
        base += n
    tabs = dict(crow=crow, hprev=hprev, hnext=hnext, rblk=rblk, s_chunk_f=s_chunk_f, s_chunk_b=s_chunk_b,
                s_first=s_first, s_last=s_last, s_islat=s_islat, s_latb=s_latb, s_ctxb=s_ctxb)
    tabs = {k: jnp.asarray(np.asarray(v, np.int32)) for k, v in tabs.items()}
    tabs["batch"] = batch
    return tabs


def _rope_tables(dec_seq):
    rows = dec_seq // GRID_W
    row = jnp.repeat(jnp.arange(rows), GRID_W).astype(F32)
    col = jnp.tile(jnp.arange(GRID_W), rows).astype(F32)
    half = HEAD_DIM // 2
    inv_freq = ROPE_THETA ** (-jnp.arange(0, half, 2, dtype=F32) / half)
    ang_r = row[:, None] * inv_freq
    ang_c = col[:, None] * inv_freq
    ang = jnp.concatenate([ang_r, ang_r, ang_c, ang_c], axis=-1)
    sign = jnp.where((jnp.arange(HEAD_DIM) & (HEAD_DIM // 4)) == 0, -1.0, 1.0).astype(F32)
    cos = jnp.concatenate([jnp.ones((ROW_TILE, HEAD_DIM), F32), jnp.cos(ang)], axis=0)
    sin = jnp.concatenate([jnp.zeros((ROW_TILE, HEAD_DIM), F32), jnp.sin(ang) * sign], axis=0)
    return cos, sin


def _dir_lanes(p):
    tail = jnp.zeros(p.shape[:-1] + (LANES - DT_COPIES * SSM_HEADS,), p.dtype)
    return jnp.concatenate([p] * DT_COPIES + [tail], axis=-1)


def _pack_w_in(w_in):
    depth = w_in.shape[0]
    c_dt = ATTN_W + 2 * KV_W + D_INNER + CONV_CH
    w_dt = w_in[:, :, c_dt:c_dt + 2 * SSM_HEADS].reshape(depth, D_MODEL, 2, SSM_HEADS)
    w_dt = _dir_lanes(w_dt).reshape(depth, D_MODEL, 2 * LANES)
    return jnp.concatenate([w_in[:, :, :c_dt], w_dt, w_in[:, :, c_dt + 2 * SSM_HEADS:]], axis=-1).astype(BF16)


def kernel(x_prompt, x_sample, c, cache_k, cache_v, state_ssd, c_ctx, w_mod, b_mod, w_in, q_norm, k_norm,
           conv_w, conv_b, dt_bias, a_log, d_skip, ssd_norm, w_attn_o, w_ssd_o, w_out, w_up,
           ffn_conv_w, ffn_conv_b, w_down):
    batch, seq, _ = x_prompt.shape
    dec_batch, dec_seq, _ = x_sample.shape
    depth = w_in.shape[0]
    past = cache_k.shape[2]
    n_ctx = batch * seq
    assert seq % ROW_TILE == 0 and dec_seq % ROW_TILE == 0 and n_ctx % dec_seq == 0
    assert 1 + dec_batch <= SUBLANES_F32

    tabs = _tables(batch, seq, dec_batch, dec_seq)
    rope_cos, rope_sin = _rope_tables(dec_seq)
    cond = jnp.concatenate([c_ctx[None, :], c, jnp.zeros((SUBLANES_F32 - 1 - dec_batch, D_MODEL), F32)], axis=0)
    mods = _mod_call(cond, w_mod, b_mod)

    w_pack = _pack_w_in(w_in)
    wa = w_attn_o.astype(BF16)
    ws = w_ssd_o.astype(BF16)
    wo = w_out.astype(BF16)
    wu = w_up.astype(BF16)
    wd = w_down.astype(BF16)
    qn = q_norm.reshape(depth, 1, HEAD_DIM)
    kn = k_norm.reshape(depth, 1, HEAD_DIM)
    cvb = conv_b.reshape(depth, 1, CONV_CH)
    fcb = ffn_conv_b.reshape(depth, 1, 2 * D_FF)
    dtb = _dir_lanes(dt_bias).reshape(depth, 1, 2 * LANES)
    alog = _dir_lanes(a_log).reshape(depth, 2, 1, LANES)
    dsk = jnp.repeat(d_skip, SSM_HEAD_DIM, axis=-1).reshape(depth, 1, D_INNER)
    nrm = ssd_norm.reshape(depth, 1, D_INNER)
    ck = cache_k.reshape(dec_batch, depth, past, KV_W)
    cv = cache_v.reshape(dec_batch, depth, past, KV_W)

    x = jnp.concatenate([x_prompt.reshape(n_ctx, D_MODEL), x_sample.reshape(dec_batch * dec_seq, D_MODEL)], axis=0)
    new_k, new_v = [], []
    new_s = None
    for l in range(depth):
        q, k, v, kf, vf, z, xbc, dt, ga, gs = _in_call(
            l, x, mods, tabs, rope_cos, rope_sin, qn, kn, conv_w, cvb, dtb, w_pack)
        attn = _attn_ctx_call(q, k, v, batch, seq)
        attn = _attn_lat_call(q, k, v, ck, cv, attn, n_ctx, dec_batch, dec_seq, l)
        yf, yb, new_s = _ssd_call(l, xbc, dt, alog, state_ssd, tabs, new_s)
        x1, h2 = _comb_call(l, tabs, attn, yf, yb, xbc, z, ga, gs, x, mods, dsk, nrm, wa, ws, wo)
        x = _ffn_call(l, tabs, h2, x1, mods, ffn_conv_w, fcb, wu, wd)
        new_k.append(kf[:n_ctx].reshape(batch, seq, H_KV, HEAD_DIM))
        new_v.append(vf[:n_ctx].reshape(batch, seq, H_KV, HEAD_DIM))
    y_prompt = x[:n_ctx].reshape(batch, seq, D_MODEL)
    y_sample = x[n_ctx:].reshape(dec_batch, dec_seq, D_MODEL)
    new_state = new_s.reshape(batch, depth, 2, SSM_HEADS, SSM_HEAD_DIM, SSM_STATE)
    return (y_prompt, y_sample, jnp.stack(new_k, axis=1), jnp.stack(new_v, axis=1), new_state)
```

```python
import functools
import math

import numpy as np
import jax
import jax.numpy as jnp
from jax import lax
from jax.experimental import pallas as pl
from jax.experimental.pallas import tpu as pltpu

F32 = jnp.float32
BF16 = jnp.bfloat16

D_MODEL = 1024
GRID_W = 64
EPS = 1e-6
H_Q = 8
H_KV = 2
HEAD_DIM = 128
ATTN_W = H_Q * HEAD_DIM
KV_W = H_KV * HEAD_DIM
ROPE_THETA = 10000.0
D_INNER = 2 * D_MODEL
SSM_HEAD_DIM = 64
SSM_HEADS = D_INNER // SSM_HEAD_DIM
SSM_GROUPS = 4
SSM_STATE = 128
SSM_CHUNK = 128
CONV_CH = D_INNER + 2 * SSM_GROUPS * SSM_STATE
D_FF = 2816
GROUP_W = (SSM_HEADS // SSM_GROUPS) * SSM_HEAD_DIM
LOG2E = 1.4426950408889634

LANES = 128
SUBLANES_F32 = 8
SUBLANES_BF16 = 16
VMEM_LIMIT_BYTES = 56 * 1024 * 1024

Q0 = 0
K0 = Q0 + ATTN_W
V0 = K0 + KV_W
Z0 = V0 + KV_W
X0 = Z0 + D_INNER
DT0 = X0 + CONV_CH
DT_COPIES = 3

ROW_TILE = 256
SSD_CHUNKS_PER_STEP = 2


def _sigmoid(x):
    return 0.5 * jnp.tanh(0.5 * x) + 0.5


def _silu(x):
    h = 0.5 * x
    return h * jnp.tanh(h) + h


def _params(n_axes):
    return pltpu.CompilerParams(dimension_semantics=("arbitrary",) * n_axes,
                                vmem_limit_bytes=VMEM_LIMIT_BYTES)


def _resident(shape, index_map):
    return pl.BlockSpec(shape, index_map, pipeline_mode=pl.Buffered(1))


def _layer_block(arr, layer, resident=False):
    shape = (None,) + tuple(arr.shape[1:])
    nz = len(arr.shape) - 1

    def index_map(*_):
        return (layer,) + (0,) * nz

    return _resident(shape, index_map) if resident else pl.BlockSpec(shape, index_map)


def _mod_kernel(cond_ref, w_ref, b_ref, o_ref):
    a = _silu(cond_ref[...]).astype(BF16)
    o_ref[0] = jnp.dot(a, w_ref[0].astype(BF16), preferred_element_type=F32) + b_ref[0]


def _mod_call(cond, w_mod, b_mod):
    depth, _, n = w_mod.shape
    tn = 1536
    return pl.pallas_call(
        _mod_kernel,
        grid=(depth, n // tn),
        in_specs=[pl.BlockSpec(cond.shape, lambda l, j: (0, 0)),
                  pl.BlockSpec((1, D_MODEL, tn), lambda l, j: (l, 0, j)),
                  pl.BlockSpec((1, 1, tn), lambda l, j: (l, 0, j))],
        out_specs=pl.BlockSpec((1, cond.shape[0], tn), lambda l, j: (l, 0, j)),
        out_shape=jax.ShapeDtypeStruct((depth, cond.shape[0], n), F32),
        compiler_params=_params(2),
        name="mod_vectors",
    )(cond, w_mod, b_mod.reshape(depth, 1, n))


def _modulate(x, shift, scale):
    ms = jnp.mean(x * x, axis=-1, keepdims=True)
    return (x * lax.rsqrt(ms + EPS)) * (1.0 + scale) + shift


def _in_kernel(crow_ref, hprev_ref, hnext_ref, rblk_ref,
               x_ref, xp_ref, xn_ref, mod_ref, cos_ref, sin_ref, qn_ref, kn_ref, cw_ref, cb_ref,
               dtb_ref, w_ref, wdt_ref, wg_ref,
               q_ref, k_ref, v_ref, kf_ref, vf_ref, z_ref, xbc_ref, dt_ref, ga_ref, gs_ref,
               s_ref):
    del rblk_ref
    i = pl.program_id(0)
    tm = x_ref.shape[0]
    halo = xp_ref.shape[0]
    mod = mod_ref[pl.ds(crow_ref[i], 1), :]
    shift = mod[:, 0:D_MODEL]
    scale = mod[:, D_MODEL:2 * D_MODEL]
    hm = _modulate(x_ref[...], shift, scale)
    hp = _modulate(xp_ref[...], shift, scale) * hprev_ref[i].astype(F32)
    hn = _modulate(xn_ref[...], shift, scale) * hnext_ref[i].astype(F32)
    h = hm.astype(BF16)
    h_ext = jnp.concatenate([hp, hm, hn], axis=0).astype(BF16)

    qkv = jnp.dot(h, w_ref[:, Q0:Z0], preferred_element_type=F32)
    cos = cos_ref[...]
    sin = sin_ref[...]
    lane = lax.broadcasted_iota(jnp.int32, (tm, HEAD_DIM), 1)
    first_quarter = (lane & (HEAD_DIM // 4)) == 0

    def norm_rope(u, w):
        un = u * lax.rsqrt(jnp.mean(u * u, axis=-1, keepdims=True) + EPS) * w
        rot = jnp.where(first_quarter, pltpu.roll(un, HEAD_DIM - HEAD_DIM // 4, 1),
                        pltpu.roll(un, HEAD_DIM // 4, 1))
        return un * cos + rot * sin

    q_scale = HEAD_DIM ** -0.5 * LOG2E
    for j in range(H_Q):
        qh = norm_rope(qkv[:, j * HEAD_DIM:(j + 1) * HEAD_DIM], qn_ref[...])
        q_ref[:, j * HEAD_DIM:(j + 1) * HEAD_DIM] = (qh * q_scale).astype(BF16)
    ones = jnp.ones((tm, HEAD_DIM), BF16)
    for j in range(H_KV):
        kh = norm_rope(qkv[:, K0 + j * HEAD_DIM:K0 + (j + 1) * HEAD_DIM], kn_ref[...])
        k_ref[:, j * HEAD_DIM:(j + 1) * HEAD_DIM] = kh.astype(BF16)
        kf_ref[:, j * HEAD_DIM:(j + 1) * HEAD_DIM] = kh
        v_ref[:, 2 * j * HEAD_DIM:(2 * j + 1) * HEAD_DIM] = qkv[:, V0 + j * HEAD_DIM:V0 + (j + 1) * HEAD_DIM].astype(BF16)
        v_ref[:, (2 * j + 1) * HEAD_DIM:(2 * j + 2) * HEAD_DIM] = ones
    vf_ref[...] = qkv[:, V0:Z0]

    z_ref[...] = _silu(jnp.dot(h, w_ref[:, Z0:X0], preferred_element_type=F32)).astype(BF16)

    s_ref[...] = jnp.dot(h_ext, w_ref[:, X0:DT0], preferred_element_type=F32)
    xc = (cb_ref[...] + cw_ref[0:1, :] * s_ref[halo - 1:halo - 1 + tm, :]
          + cw_ref[1:2, :] * s_ref[halo:halo + tm, :]
          + cw_ref[2:3, :] * s_ref[halo + 1:halo + 1 + tm, :])
    xbc_ref[...] = _silu(xc).astype(BF16)

    dt_raw = jnp.dot(h, wdt_ref[...], preferred_element_type=F32) + dtb_ref[...]
    dt_ref[...] = jnp.maximum(dt_raw, 0.0) + jnp.log1p(jnp.exp(-jnp.abs(dt_raw)))

    ga_ref[...] = _sigmoid(jnp.dot(h, wg_ref[:, 0:D_MODEL], preferred_element_type=F32)).astype(BF16)
    gs_ref[...] = _sigmoid(jnp.dot(h, wg_ref[:, D_MODEL:2 * D_MODEL], preferred_element_type=F32)).astype(BF16)


def _in_call(layer, x, mods, tabs, rope_cos, rope_sin, q_norm, k_norm, conv_w, conv_b, dt_bias, w_main, w_dt,
             w_gates):
    n_tok = x.shape[0]
    tm = ROW_TILE
    halo = SUBLANES_F32
    n_tiles = n_tok // tm
    hb = tm // halo
    last_hb = n_tok // halo - 1

    def row(width):
        return pl.BlockSpec((tm, width), lambda i, *_: (i, 0))

    grid_spec = pltpu.PrefetchScalarGridSpec(
        num_scalar_prefetch=4,
        grid=(n_tiles,),
        in_specs=[
            row(D_MODEL),
            pl.BlockSpec((halo, D_MODEL), lambda i, *_: (jnp.maximum(i * hb - 1, 0), 0)),
            pl.BlockSpec((halo, D_MODEL), lambda i, *_: (jnp.minimum((i + 1) * hb, last_hb), 0)),
            _layer_block(mods, layer),
            pl.BlockSpec((tm, HEAD_DIM), lambda i, c, p, n, r: (r[i], 0)),
            pl.BlockSpec((tm, HEAD_DIM), lambda i, c, p, n, r: (r[i], 0)),
            _layer_block(q_norm, layer), _layer_block(k_norm, layer),
            _layer_block(conv_w, layer), _layer_block(conv_b, layer), _layer_block(dt_bias, layer),
            _layer_block(w_main, layer, resident=True),
            _layer_block(w_dt, layer, resident=True),
            _layer_block(w_gates, layer, resident=True),
        ],
        out_specs=[row(ATTN_W), row(KV_W), row(2 * KV_W), row(KV_W), row(KV_W), row(D_INNER),
                   row(CONV_CH), row(2 * LANES), row(D_MODEL), row(D_MODEL)],
        scratch_shapes=[pltpu.VMEM((tm + 2 * halo, CONV_CH), F32)],
    )
    sds = jax.ShapeDtypeStruct
    return pl.pallas_call(
        _in_kernel,
        grid_spec=grid_spec,
        out_shape=[sds((n_tok, ATTN_W), BF16), sds((n_tok, KV_W), BF16), sds((n_tok, 2 * KV_W), BF16),
                   sds((n_tok, KV_W), F32), sds((n_tok, KV_W), F32), sds((n_tok, D_INNER), BF16),
                   sds((n_tok, CONV_CH), BF16), sds((n_tok, 2 * LANES), F32),
                   sds((n_tok, D_MODEL), BF16), sds((n_tok, D_MODEL), BF16)],
        compiler_params=_params(1),
        name="in_proj",
    )(tabs["crow"], tabs["hprev"], tabs["hnext"], tabs["rblk"],
      x, x, x, mods, rope_cos, rope_sin, q_norm, k_norm, conv_w, conv_b, dt_bias, w_main, w_dt, w_gates)


def _attn_kernel(*refs, has_cache):
    if has_cache:
        q_ref, k_ref, v_ref, kc_ref, vc_ref, _, o_ref, kt_ref, vcx_ref = refs
        past = kc_ref.shape[0]
    else:
        q_ref, k_ref, v_ref, o_ref, kt_ref = refs
        past = 0
    rep = H_Q // H_KV

    @pl.when(pl.program_id(1) == 0)
    def _():
        if has_cache:
            kt_ref[:, 0:past] = kc_ref[...].T.astype(BF16)
            for g in range(H_KV):
                vcx_ref[:, 2 * g * HEAD_DIM:(2 * g + 1) * HEAD_DIM] = vc_ref[:, g * HEAD_DIM:(g + 1) * HEAD_DIM].astype(BF16)
                vcx_ref[:, (2 * g + 1) * HEAD_DIM:(2 * g + 2) * HEAD_DIM] = jnp.ones((past, HEAD_DIM), BF16)
        kt_ref[:, past:] = k_ref[...].astype(F32).T.astype(BF16)

    for j in range(H_Q):
        g = j // rep
        q = q_ref[:, j * HEAD_DIM:(j + 1) * HEAD_DIM]
        s = jnp.dot(q, kt_ref[g * HEAD_DIM:(g + 1) * HEAD_DIM, :], preferred_element_type=F32)
        p = jnp.exp2(s - jnp.max(s, axis=-1, keepdims=True)).astype(BF16)
        o = jnp.dot(p[:, past:], v_ref[:, 2 * g * HEAD_DIM:(2 * g + 2) * HEAD_DIM], preferred_element_type=F32)
        if has_cache:
            o = o + jnp.dot(p[:, 0:past], vcx_ref[:, 2 * g * HEAD_DIM:(2 * g + 2) * HEAD_DIM],
                            preferred_element_type=F32)
        o_ref[:, j * HEAD_DIM:(j + 1) * HEAD_DIM] = (o[:, 0:HEAD_DIM] / o[:, HEAD_DIM:2 * HEAD_DIM]).astype(BF16)


def _attn_ctx_call(q, k, v, batch, seq):
    n_tok = q.shape[0]
    return pl.pallas_call(
        functools.partial(_attn_kernel, has_cache=False),
        grid=(batch, 1),
        in_specs=[pl.BlockSpec((seq, ATTN_W), lambda b, t: (b, 0)),
                  pl.BlockSpec((seq, KV_W), lambda b, t: (b, 0)),
                  pl.BlockSpec((seq, 2 * KV_W), lambda b, t: (b, 0))],
        out_specs=pl.BlockSpec((seq, ATTN_W), lambda b, t: (b, 0)),
        out_shape=jax.ShapeDtypeStruct((n_tok, ATTN_W), BF16),
        scratch_shapes=[pltpu.VMEM((KV_W, seq), BF16)],
        compiler_params=_params(2),
        name="attn_context",
    )(q, k, v)


def _attn_lat_call(q, k, v, cache_k, cache_v, attn, n_ctx_tok, dec_batch, dec_seq, layer):
    tq = ROW_TILE
    t0 = n_ctx_tok // tq
    s0 = n_ctx_tok // dec_seq
    tiles = dec_seq // tq
    past = cache_k.shape[2]
    return pl.pallas_call(
        functools.partial(_attn_kernel, has_cache=True),
        grid=(dec_batch, tiles),
        in_specs=[pl.BlockSpec((tq, ATTN_W), lambda b, t: (t0 + b * tiles + t, 0)),
                  pl.BlockSpec((dec_seq, KV_W), lambda b, t: (s0 + b, 0)),
                  pl.BlockSpec((dec_seq, 2 * KV_W), lambda b, t: (s0 + b, 0)),
                  pl.BlockSpec((None, None, past, KV_W), lambda b, t: (b, layer, 0, 0)),
                  pl.BlockSpec((None, None, past, KV_W), lambda b, t: (b, layer, 0, 0)),
                  pl.BlockSpec(memory_space=pl.ANY)],
        out_specs=pl.BlockSpec((tq, ATTN_W), lambda b, t: (t0 + b * tiles + t, 0)),
        out_shape=jax.ShapeDtypeStruct(attn.shape, attn.dtype),
        scratch_shapes=[pltpu.VMEM((KV_W, past + dec_seq), BF16),
                        pltpu.VMEM((past, 2 * KV_W), BF16)],
        input_output_aliases={5: 0},
        compiler_params=_params(2),
        name="attn_latent",
    )(q, k, v, cache_k, cache_v, attn)


def _bf16_split3(x):
    c1 = x.astype(BF16).astype(F32)
    r = x - c1
    c2 = r.astype(BF16).astype(F32)
    c3 = (r - c2).astype(BF16).astype(F32)
    return c1, c2, c3


def _ssd_direction(x_ref, dt_ref, alog, h_ref, y_ref, ecol_ref, ew_ref, d, r0):
    q = SSM_CHUNK
    heads_per_group = SSM_HEADS // SSM_GROUPS
    dt = dt_ref[r0:r0 + q, :]
    a = -jnp.exp(alog)
    row = lax.broadcasted_iota(jnp.int32, (q, q), 0)
    col = lax.broadcasted_iota(jnp.int32, (q, q), 1)
    mask = (row >= col) if d == 0 else (row <= col)
    ones_tri = jnp.where(mask, 1.0, 0.0).astype(BF16)
    lane = lax.broadcasted_iota(jnp.int32, (q, LANES), 1)

    def pick_split(parts):
        return jnp.where(lane < SSM_HEADS, parts[0],
                         jnp.where(lane < 2 * SSM_HEADS, parts[1],
                                   jnp.where(lane < 3 * SSM_HEADS, parts[2], 0.0))).astype(BF16)

    da = jnp.concatenate(_bf16_split3(dt * a), axis=1).astype(BF16)
    cs = jnp.dot(ones_tri, da, preferred_element_type=F32)
    cum = (cs[:, 0:LANES] + cs[:, LANES:2 * LANES] + cs[:, 2 * LANES:3 * LANES]) * LOG2E
    cum_t = cum.T
    dt_t = dt.T
    total = cum[q - 1:q, :] if d == 0 else cum[0:1, :]
    state_w = dt * jnp.exp2(total - cum)
    col_all = jnp.dot(pick_split(_bf16_split3(cum)), ecol_ref[...], preferred_element_type=F32)
    sw_all = jnp.dot(pick_split(_bf16_split3(state_w)), ew_ref[...], preferred_element_type=F32)
    lo = lane < SSM_HEAD_DIM
    lo_row = lo[0:1, :]
    last = q - 1 if d == 0 else 0
    row2 = lax.broadcasted_iota(jnp.int32, (2 * q, LANES), 0)
    lane2 = lax.broadcasted_iota(jnp.int32, (2 * q, LANES), 1)
    pair_mask = jnp.where((row2 < q) == (lane2 < SSM_HEAD_DIM), 1.0, 0.0).astype(BF16)

    for g in range(SSM_GROUPS):
        b0 = D_INNER + g * SSM_STATE
        c0 = D_INNER + SSM_GROUPS * SSM_STATE + g * SSM_STATE
        bg_t = x_ref[r0:r0 + q, b0:b0 + SSM_STATE].astype(F32).T.astype(BF16)
        cg = x_ref[r0:r0 + q, c0:c0 + SSM_STATE]
        cb = jnp.dot(cg, bg_t, preferred_element_type=F32)
        h_t = h_ref[d, g]
        y_off = jnp.dot(cg, h_t.astype(BF16), preferred_element_type=F32)
        xw_parts = []
        dec_parts = []
        for pp in range(heads_per_group // 2):
            h0 = g * heads_per_group + 2 * pp
            h1 = h0 + 1
            col0 = col_all[:, h0 * LANES:(h0 + 1) * LANES]
            col1 = col_all[:, h1 * LANES:(h1 + 1) * LANES]
            w0 = cb * jnp.exp2(jnp.where(mask, col0 - cum_t[h0:h0 + 1, :], -jnp.inf)) * dt_t[h0:h0 + 1, :]
            w1 = cb * jnp.exp2(jnp.where(mask, col1 - cum_t[h1:h1 + 1, :], -jnp.inf)) * dt_t[h1:h1 + 1, :]
            x0 = h0 * SSM_HEAD_DIM
            xpair = x_ref[r0:r0 + q, x0:x0 + LANES]
            w01 = jnp.concatenate([w0.astype(BF16), w1.astype(BF16)], axis=1)
            x01 = jnp.concatenate([xpair, xpair], axis=0) * pair_mask
            y_diag = jnp.dot(w01, x01, preferred_element_type=F32)
            colsel = jnp.where(lo, col0, col1)
            y_pair = y_diag + y_off[:, pp * LANES:(pp + 1) * LANES] * jnp.exp2(colsel)
            y_ref[r0:r0 + q, x0:x0 + LANES] = y_pair.astype(BF16)
            xw_parts.append((xpair.astype(F32) * sw_all[:, x0:x0 + LANES]).astype(BF16))
            dec_parts.append(jnp.exp2(jnp.where(lo_row, col0[last:last + 1, :], col1[last:last + 1, :])))
        xw = jnp.concatenate(xw_parts, axis=1)
        decay = jnp.concatenate(dec_parts, axis=1)
        h_ref[d, g] = h_t * decay + jnp.dot(bg_t, xw, preferred_element_type=F32)


def _ssd_kernel(chf_ref, chb_ref, first_ref, last_ref, islat_ref, latb_ref, ctxb_ref,
                xf_ref, xb_ref, dtf_ref, dtb_ref, alog_ref, init_ref, ecol_ref, ew_ref,
                yf_ref, yb_ref, fin_ref, h_ref):
    del chf_ref, chb_ref, latb_ref, ctxb_ref
    step = pl.program_id(0)

    @pl.when((first_ref[step] == 1) & (islat_ref[step] == 0))
    def _():
        h_ref[...] = jnp.zeros(h_ref.shape, F32)

    @pl.when((first_ref[step] == 1) & (islat_ref[step] == 1))
    def _():
        for d in range(2):
            for g in range(SSM_GROUPS):
                h_ref[d, g] = init_ref[d, g * GROUP_W:(g + 1) * GROUP_W, :].T

    for sub in range(SSD_CHUNKS_PER_STEP):
        _ssd_direction(xf_ref, dtf_ref, alog_ref[0], h_ref, yf_ref, ecol_ref, ew_ref, 0, sub * SSM_CHUNK)
        _ssd_direction(xb_ref, dtb_ref, alog_ref[1], h_ref, yb_ref, ecol_ref, ew_ref, 1,
                       (SSD_CHUNKS_PER_STEP - 1 - sub) * SSM_CHUNK)

    @pl.when((last_ref[step] == 1) & (islat_ref[step] == 0))
    def _():
        for d in range(2):
            for g in range(SSM_GROUPS):
                fin_ref[d, g * GROUP_W:(g + 1) * GROUP_W, :] = h_ref[d, g].T


def _expansion_matrices():
    r = np.arange(LANES)
    head = r % SSM_HEADS
    live = r < DT_COPIES * SSM_HEADS
    ecol = (live[:, None] & (head[:, None] == (np.arange(SSM_HEADS * LANES) // LANES)[None, :]))
    ew = (live[:, None] & (head[:, None] == (np.arange(D_INNER) // SSM_HEAD_DIM)[None, :]))
    return jnp.asarray(ecol, BF16), jnp.asarray(ew, BF16)


def _ssd_call(layer, xbc, dt, a_log, state_ssd, tabs, new_s):
    n_tok = xbc.shape[0]
    q = SSD_CHUNKS_PER_STEP * SSM_CHUNK
    n_steps = n_tok // q
    hpn = SSM_HEADS * SSM_HEAD_DIM
    dec_batch, depth = state_ssd.shape[0], state_ssd.shape[1]
    batch = int(tabs["batch"])
    init = state_ssd.reshape(dec_batch, depth, 2, hpn, SSM_STATE)
    ecol, ew = _expansion_matrices()

    in_specs = [
        pl.BlockSpec((q, CONV_CH), lambda j, cf, cb, *_: (cf[j], 0)),
        pl.BlockSpec((q, CONV_CH), lambda j, cf, cb, *_: (cb[j], 0)),
        pl.BlockSpec((q, LANES), lambda j, cf, cb, *_: (cf[j], 0)),
        pl.BlockSpec((q, LANES), lambda j, cf, cb, *_: (cb[j], 1)),
        _layer_block(a_log, layer),
        pl.BlockSpec((None, None, 2, hpn, SSM_STATE),
                     lambda j, cf, cb, fi, la, il, lb, xb: (lb[j], layer, 0, 0, 0)),
        _resident(ecol.shape, lambda j, *_: (0, 0)),
        _resident(ew.shape, lambda j, *_: (0, 0)),
    ]
    args = [xbc, xbc, dt, dt, a_log, init, ecol, ew]
    aliases = {}
    if new_s is not None:
        in_specs.append(pl.BlockSpec(memory_space=pl.ANY))
        args.append(new_s)
        aliases = {7 + len(args) - 1: 2}
    grid_spec = pltpu.PrefetchScalarGridSpec(
        num_scalar_prefetch=7,
        grid=(n_steps,),
        in_specs=in_specs,
        out_specs=[
            pl.BlockSpec((q, D_INNER), lambda j, cf, cb, *_: (cf[j], 0)),
            pl.BlockSpec((q, D_INNER), lambda j, cf, cb, *_: (cb[j], 0)),
            pl.BlockSpec((None, None, 2, hpn, SSM_STATE),
                         lambda j, cf, cb, fi, la, il, lb, xb: (xb[j], layer, 0, 0, 0)),
        ],
        scratch_shapes=[pltpu.VMEM((2, SSM_GROUPS, SSM_STATE, GROUP_W), F32)],
    )

    def body(*refs):
        if new_s is not None:
            refs = refs[:15] + refs[16:]
        _ssd_kernel(*refs)

    return pl.pallas_call(
        body,
        grid_spec=grid_spec,
        out_shape=[jax.ShapeDtypeStruct((n_tok, D_INNER), BF16),
                   jax.ShapeDtypeStruct((n_tok, D_INNER), BF16),
                   jax.ShapeDtypeStruct((batch, depth, 2, hpn, SSM_STATE), F32)],
        input_output_aliases=aliases,
        compiler_params=_params(1),
        name="ssd_scan",
    )(tabs["s_chunk_f"], tabs["s_chunk_b"], tabs["s_first"], tabs["s_last"], tabs["s_islat"],
      tabs["s_latb"], tabs["s_ctxb"], *args)


def _comb_kernel(crow_ref, attn_ref, yf_ref, yb_ref, xs_ref, z_ref, ga_ref, gs_ref, x_ref, mod_ref,
                 dskip_ref, nw_ref, wa_ref, ws_ref, wo_ref, x1_ref, h2_ref):
    i = pl.program_id(0)
    mod = mod_ref[pl.ds(crow_ref[i], 1), :]
    gate1 = mod[:, 2 * D_MODEL:3 * D_MODEL]
    shift2 = mod[:, 3 * D_MODEL:4 * D_MODEL]
    scale2 = mod[:, 4 * D_MODEL:5 * D_MODEL]
    y = yf_ref[...].astype(F32) + yb_ref[...].astype(F32) + dskip_ref[...] * xs_ref[...].astype(F32)
    y = y * z_ref[...].astype(F32)
    yn = (y * lax.rsqrt(jnp.mean(y * y, axis=-1, keepdims=True) + EPS)) * nw_ref[...]
    merged = (ga_ref[...].astype(F32) * jnp.dot(attn_ref[...], wa_ref[...], preferred_element_type=F32)
              + gs_ref[...].astype(F32) * jnp.dot(yn.astype(BF16), ws_ref[...], preferred_element_type=F32))
    x1 = x_ref[...] + gate1 * jnp.dot(merged.astype(BF16), wo_ref[...], preferred_element_type=F32)
    x1_ref[...] = x1
    h2_ref[...] = _modulate(x1, shift2, scale2).astype(BF16)


def _comb_call(layer, tabs, attn, yf, yb, xbc, z, ga, gs, x, mods, d_skip, ssd_norm, w_attn_o, w_ssd_o, w_out):
    n_tok = x.shape[0]
    tm = ROW_TILE

    def row(width):
        return pl.BlockSpec((tm, width), lambda i, *_: (i, 0))

    grid_spec = pltpu.PrefetchScalarGridSpec(
        num_scalar_prefetch=1,
        grid=(n_tok // tm,),
        in_specs=[row(ATTN_W), row(D_INNER), row(D_INNER), row(D_INNER), row(D_INNER),
                  row(D_MODEL), row(D_MODEL), row(D_MODEL),
                  _layer_block(mods, layer), _layer_block(d_skip, layer), _layer_block(ssd_norm, layer),
                  _layer_block(w_attn_o, layer, resident=True),
                  _layer_block(w_ssd_o, layer, resident=True),
                  _layer_block(w_out, layer, resident=True)],
        out_specs=[row(D_MODEL), row(D_MODEL)],
    )
    return pl.pallas_call(
        _comb_kernel,
        grid_spec=grid_spec,
        out_shape=[jax.ShapeDtypeStruct((n_tok, D_MODEL), F32),
                   jax.ShapeDtypeStruct((n_tok, D_MODEL), BF16)],
        compiler_params=_params(1),
        name="combine",
    )(tabs["crow"], attn, yf, yb, xbc, z, ga, gs, x, mods, d_skip, ssd_norm, w_attn_o, w_ssd_o, w_out)


FFN_COL_CHUNK = D_FF // 2


def _ffn_kernel(crow_ref, hprev_ref, hnext_ref,
                h_ref, hp_ref, hn_ref, x_ref, mod_ref, cw_ref, cb_ref, wu_ref, wd_ref,
                o_ref, sv_ref, sg_ref):
    i = pl.program_id(0)
    tm = h_ref.shape[0]
    halo = hp_ref.shape[0]
    mod = mod_ref[pl.ds(crow_ref[i], 1), :]
    gate2 = mod[:, 5 * D_MODEL:6 * D_MODEL]
    hp = (hp_ref[...].astype(F32) * hprev_ref[i].astype(F32)).astype(BF16)
    hn = (hn_ref[...].astype(F32) * hnext_ref[i].astype(F32)).astype(BF16)
    h_ext = jnp.concatenate([hp, h_ref[...], hn], axis=0)
    acc = jnp.zeros((tm, D_MODEL), F32)
    cc = FFN_COL_CHUNK
    for c in range(D_FF // cc):
        halves = []
        for s_ref, base in ((sv_ref, c * cc), (sg_ref, D_FF + c * cc)):
            s_ref[...] = jnp.dot(h_ext, wu_ref[:, base:base + cc], preferred_element_type=F32)
            u = (cb_ref[:, base:base + cc]
                 + cw_ref[0:1, base:base + cc] * s_ref[halo - 1:halo - 1 + tm, :]
                 + cw_ref[1:2, base:base + cc] * s_ref[halo:halo + tm, :]
                 + cw_ref[2:3, base:base + cc] * s_ref[halo + 1:halo + 1 + tm, :])
            halves.append(u)
        act = (_silu(halves[1]) * halves[0]).astype(BF16)
        acc = acc + jnp.dot(act, wd_ref[c * cc:(c + 1) * cc, :], preferred_element_type=F32)
    o_ref[...] = x_ref[...] + gate2 * acc


def _ffn_call(layer, tabs, h2, x1, mods, ffn_conv_w, ffn_conv_b, w_up, w_down):
    n_tok = x1.shape[0]
    tm = ROW_TILE
    halo = SUBLANES_BF16
    hb = tm // halo
    last_hb = n_tok // halo - 1

    grid_spec = pltpu.PrefetchScalarGridSpec(
        num_scalar_prefetch=3,
        grid=(n_tok // tm,),
        in_specs=[pl.BlockSpec((tm, D_MODEL), lambda i, *_: (i, 0)),
                  pl.BlockSpec((halo, D_MODEL), lambda i, *_: (jnp.maximum(i * hb - 1, 0), 0)),
                  pl.BlockSpec((halo, D_MODEL), lambda i, *_: (jnp.minimum((i + 1) * hb, last_hb), 0)),
                  pl.BlockSpec((tm, D_MODEL), lambda i, *_: (i, 0)),
                  _layer_block(mods, layer), _layer_block(ffn_conv_w, layer), _layer_block(ffn_conv_b, layer),
                  _layer_block(w_up, layer, resident=True),
                  _layer_block(w_down, layer, resident=True)],
        out_specs=pl.BlockSpec((tm, D_MODEL), lambda i, *_: (i, 0)),
        scratch_shapes=[pltpu.VMEM((tm + 2 * halo, FFN_COL_CHUNK), F32),
                        pltpu.VMEM((tm + 2 * halo, FFN_COL_CHUNK), F32)],
    )
    return pl.pallas_call(
        _ffn_kernel,
        grid_spec=grid_spec,
        out_shape=jax.ShapeDtypeStruct((n_tok, D_MODEL), F32),
        compiler_params=_params(1),
        name="conv_ffn",
    )(tabs["crow"], tabs["hprev"], tabs["hnext"], h2, h2, h2, x1, mods, ffn_conv_w, ffn_conv_b, w_up, w_down)


def _tables(batch, seq, dec_batch, dec_seq):
    tm = ROW_TILE
    crow, hprev, hnext, rblk = [], [], [], []
    for b in range(batch):
        for t in range(seq // tm):
            crow.append(0)
            hprev.append(int(t > 0))
            hnext.append(int(t < seq // tm - 1))
            rblk.append(0)
    for b in range(dec_batch):
        for t in range(dec_seq // tm):
            crow.append(1 + b)
            hprev.append(int(t > 0))
            hnext.append(int(t < dec_seq // tm - 1))
            rblk.append(1 + t)
    q = SSD_CHUNKS_PER_STEP * SSM_CHUNK
    assert seq % q == 0 and dec_seq % q == 0
    seqs = [(False, b, seq // q) for b in range(batch)] + [(True, b, dec_seq // q) for b in range(dec_batch)]
    s_chunk_f, s_chunk_b, s_first, s_last, s_islat, s_latb, s_ctxb = [], [], [], [], [], [], []
    base = 0
    for is_lat, b, n in seqs:
        for c in range(n):
            s_chunk_f.append(base + c)
            s_chunk_b.append(base + n - 1 - c)
            s_first.append(int(c == 0))
            s_last.append(int(c == n - 1))
            s_islat.append(int(is_lat))
            s_latb.append(b if is_lat else 0)
            s_ctxb.append(batch - 1 if is_lat else b)
        base += n
    tabs = dict(crow=crow, hprev=hprev, hnext=hnext, rblk=rblk, s_chunk_f=s_chunk_f, s_chunk_b=s_chunk_b,
                s_first=s_first, s_last=s_last, s_islat=s_islat, s_latb=s_latb, s_ctxb=s_ctxb)
    tabs = {k: jnp.asarray(np.asarray(v, np.int32)) for k, v in tabs.items()}
    tabs["batch"] = batch
    return tabs


def _rope_tables(dec_seq):
    rows = dec_seq // GRID_W
    row = jnp.repeat(jnp.arange(rows), GRID_W).astype(F32)
    col = jnp.tile(jnp.arange(GRID_W), rows).astype(F32)
    half = HEAD_DIM // 2
    inv_freq = ROPE_THETA ** (-jnp.arange(0, half, 2, dtype=F32) / half)
    ang_r = row[:, None] * inv_freq
    ang_c = col[:, None] * inv_freq
    ang = jnp.concatenate([ang_r, ang_r, ang_c, ang_c], axis=-1)
    sign = jnp.where((jnp.arange(HEAD_DIM) & (HEAD_DIM // 4)) == 0, -1.0, 1.0).astype(F32)
    cos = jnp.concatenate([jnp.ones((ROW_TILE, HEAD_DIM), F32), jnp.cos(ang)], axis=0)
    sin = jnp.concatenate([jnp.zeros((ROW_TILE, HEAD_DIM), F32), jnp.sin(ang) * sign], axis=0)
    return cos, sin


def _dir_lanes(p):
    tail = jnp.zeros(p.shape[:-1] + (LANES - DT_COPIES * SSM_HEADS,), p.dtype)
    return jnp.concatenate([p] * DT_COPIES + [tail], axis=-1)


def _split_w_in(w_in):
    depth = w_in.shape[0]
    w_dt = w_in[:, :, DT0:DT0 + 2 * SSM_HEADS].reshape(depth, D_MODEL, 2, SSM_HEADS)
    w_dt = _dir_lanes(w_dt).reshape(depth, D_MODEL, 2 * LANES)
    return (w_in[:, :, :DT0].astype(BF16), w_dt.astype(BF16), w_in[:, :, DT0 + 2 * SSM_HEADS:].astype(BF16))


def kernel(x_prompt, x_sample, c, cache_k, cache_v, state_ssd, c_ctx, w_mod, b_mod, w_in, q_norm, k_norm,
           conv_w, conv_b, dt_bias, a_log, d_skip, ssd_norm, w_attn_o, w_ssd_o, w_out, w_up,
           ffn_conv_w, ffn_conv_b, w_down):
    batch, seq, _ = x_prompt.shape
    dec_batch, dec_seq, _ = x_sample.shape
    depth = w_in.shape[0]
    past = cache_k.shape[2]
    n_ctx = batch * seq
    assert seq % ROW_TILE == 0 and dec_seq % ROW_TILE == 0 and n_ctx % dec_seq == 0
    assert 1 + dec_batch <= SUBLANES_F32

    tabs = _tables(batch, seq, dec_batch, dec_seq)
    rope_cos, rope_sin = _rope_tables(dec_seq)
    cond = jnp.concatenate([c_ctx[None, :], c, jnp.zeros((SUBLANES_F32 - 1 - dec_batch, D_MODEL), F32)], axis=0)
    mods = _mod_call(cond, w_mod, b_mod)

    w_main, w_dt, w_gates = _split_w_in(w_in)
    wa = w_attn_o.astype(BF16)
    ws = w_ssd_o.astype(BF16)
    wo = w_out.astype(BF16)
    wu = w_up.astype(BF16)
    wd = w_down.astype(BF16)
    qn = q_norm.reshape(depth, 1, HEAD_DIM)
    kn = k_norm.reshape(depth, 1, HEAD_DIM)
    cvb = conv_b.reshape(depth, 1, CONV_CH)
    fcb = ffn_conv_b.reshape(depth, 1, 2 * D_FF)
    dtb = _dir_lanes(dt_bias).reshape(depth, 1, 2 * LANES)
    alog = _dir_lanes(a_log).reshape(depth, 2, 1, LANES)
    dsk = jnp.repeat(d_skip, SSM_HEAD_DIM, axis=-1).reshape(depth, 1, D_INNER)
    nrm = ssd_norm.reshape(depth, 1, D_INNER)
    ck = cache_k.reshape(dec_batch, depth, past, KV_W)
    cv = cache_v.reshape(dec_batch, depth, past, KV_W)

    x = jnp.concatenate([x_prompt.reshape(n_ctx, D_MODEL), x_sample.reshape(dec_batch * dec_seq, D_MODEL)], axis=0)
    new_k, new_v = [], []
    new_s = None
    for l in range(depth):
        q, k, v, kf, vf, z, xbc, dt, ga, gs = _in_call(
            l, x, mods, tabs, rope_cos, rope_sin, qn, kn, conv_w, cvb, dtb, w_main, w_dt, w_gates)
        attn = _attn_ctx_call(q, k, v, batch, seq)
        attn = _attn_lat_call(q, k, v, ck, cv, attn, n_ctx, dec_batch, dec_seq, l)
        yf, yb, new_s = _ssd_call(l, xbc, dt, alog, state_ssd, tabs, new_s)
        x1, h2 = _comb_call(l, tabs, attn, yf, yb, xbc, z, ga, gs, x, mods, dsk, nrm, wa, ws, wo)
        x = _ffn_call(l, tabs, h2, x1, mods, ffn_conv_w, fcb, wu, wd)
        new_k.append(kf[:n_ctx].reshape(batch, seq, H_KV, HEAD_DIM))
        new_v.append(vf[:n_ctx].reshape(batch, seq, H_KV, HEAD_DIM))
    y_prompt = x[:n_ctx].reshape(batch, seq, D_MODEL)
    y_sample = x[n_ctx:].reshape(dec_batch, dec_seq, D_MODEL)
    new_state = new_s.reshape(batch, depth, 2, SSM_HEADS, SSM_HEAD_DIM, SSM_STATE)
    return (y_prompt, y_sample, jnp.stack(new_k, axis=1), jnp.stack(new_v, axis=1), new_state)
```

```python
import functools
import math

import numpy as np
import jax
import jax.numpy as jnp
from jax import lax
from jax.experimental import pallas as pl
from jax.experimental.pallas import tpu as pltpu

F32 = jnp.float32
BF16 = jnp.bfloat16

D_MODEL = 1024
GRID_W = 64
EPS = 1e-6
H_Q = 8
H_KV = 2
HEAD_DIM = 128
ATTN_W = H_Q * HEAD_DIM
KV_W = H_KV * HEAD_DIM
ROPE_THETA = 10000.0
D_INNER = 2 * D_MODEL
SSM_HEAD_DIM = 64
SSM_HEADS = D_INNER // SSM_HEAD_DIM
SSM_GROUPS = 4
SSM_STATE = 128
SSM_CHUNK = 128
CONV_CH = D_INNER + 2 * SSM_GROUPS * SSM_STATE
D_FF = 2816
GROUP_W = (SSM_HEADS // SSM_GROUPS) * SSM_HEAD_DIM
LOG2E = 1.4426950408889634

LANES = 128
SUBLANES_F32 = 8
SUBLANES_BF16 = 16
VMEM_LIMIT_BYTES = 56 * 1024 * 1024

Q0 = 0
K0 = Q0 + ATTN_W
V0 = K0 + KV_W
Z0 = V0 + KV_W
X0 = Z0 + D_INNER
DT0 = X0 + CONV_CH
DT_COPIES = 3

ROW_TILE = 256
SSD_CHUNKS_PER_STEP = 2
ATTN_Q_TILE = 512


def _sigmoid(x):
    return 0.5 * jnp.tanh(0.5 * x) + 0.5


def _silu(x):
    h = 0.5 * x
    return h * jnp.tanh(h) + h


def _params(n_axes):
    return pltpu.CompilerParams(dimension_semantics=("arbitrary",) * n_axes,
                                vmem_limit_bytes=VMEM_LIMIT_BYTES)


def _resident(shape, index_map):
    return pl.BlockSpec(shape, index_map, pipeline_mode=pl.Buffered(1))


def _layer_block(arr, layer, resident=False):
    shape = (None,) + tuple(arr.shape[1:])
    nz = len(arr.shape) - 1

    def index_map(*_):
        return (layer,) + (0,) * nz

    return _resident(shape, index_map) if resident else pl.BlockSpec(shape, index_map)


def _mod_kernel(cond_ref, w_ref, b_ref, o_ref):
    a = _silu(cond_ref[...]).astype(BF16)
    o_ref[0] = jnp.dot(a, w_ref[0].astype(BF16), preferred_element_type=F32) + b_ref[0]


def _mod_call(cond, w_mod, b_mod):
    depth, _, n = w_mod.shape
    tn = 1536
    return pl.pallas_call(
        _mod_kernel,
        grid=(depth, n // tn),
        in_specs=[pl.BlockSpec(cond.shape, lambda l, j: (0, 0)),
                  pl.BlockSpec((1, D_MODEL, tn), lambda l, j: (l, 0, j)),
                  pl.BlockSpec((1, 1, tn), lambda l, j: (l, 0, j))],
        out_specs=pl.BlockSpec((1, cond.shape[0], tn), lambda l, j: (l, 0, j)),
        out_shape=jax.ShapeDtypeStruct((depth, cond.shape[0], n), F32),
        compiler_params=_params(2),
        name="mod_vectors",
    )(cond, w_mod, b_mod.reshape(depth, 1, n))


def _modulate(x, shift, scale):
    ms = jnp.mean(x * x, axis=-1, keepdims=True)
    return (x * lax.rsqrt(ms + EPS)) * (1.0 + scale) + shift


def _in_kernel(crow_ref, hprev_ref, hnext_ref, rblk_ref,
               x_ref, xp_ref, xn_ref, mod_ref, cos_ref, sin_ref, qn_ref, kn_ref, cw_ref, cb_ref,
               dtb_ref, w_ref, wdt_ref, wg_ref,
               q_ref, k_ref, v_ref, kf_ref, vf_ref, z_ref, xbc_ref, dt_ref, ga_ref, gs_ref,
               s_ref):
    del rblk_ref
    i = pl.program_id(0)
    tm = x_ref.shape[0]
    halo = xp_ref.shape[0]
    mod = mod_ref[pl.ds(crow_ref[i], 1), :]
    shift = mod[:, 0:D_MODEL]
    scale = mod[:, D_MODEL:2 * D_MODEL]
    hm = _modulate(x_ref[...], shift, scale)
    hp = _modulate(xp_ref[...], shift, scale) * hprev_ref[i].astype(F32)
    hn = _modulate(xn_ref[...], shift, scale) * hnext_ref[i].astype(F32)
    h = hm.astype(BF16)
    h_ext = jnp.concatenate([hp, hm, hn], axis=0).astype(BF16)

    qkv = jnp.dot(h, w_ref[:, Q0:Z0], preferred_element_type=F32)
    cos = cos_ref[...]
    sin = sin_ref[...]
    lane = lax.broadcasted_iota(jnp.int32, (tm, HEAD_DIM), 1)
    first_quarter = (lane & (HEAD_DIM // 4)) == 0

    def norm_rope(u, w):
        un = u * lax.rsqrt(jnp.mean(u * u, axis=-1, keepdims=True) + EPS) * w
        rot = jnp.where(first_quarter, pltpu.roll(un, HEAD_DIM - HEAD_DIM // 4, 1),
                        pltpu.roll(un, HEAD_DIM // 4, 1))
        return un * cos + rot * sin

    q_scale = HEAD_DIM ** -0.5 * LOG2E
    for j in range(H_Q):
        qh = norm_rope(qkv[:, j * HEAD_DIM:(j + 1) * HEAD_DIM], qn_ref[...])
        q_ref[:, j * HEAD_DIM:(j + 1) * HEAD_DIM] = (qh * q_scale).astype(BF16)
    ones = jnp.ones((tm, HEAD_DIM), BF16)
    for j in range(H_KV):
        kh = norm_rope(qkv[:, K0 + j * HEAD_DIM:K0 + (j + 1) * HEAD_DIM], kn_ref[...])
        k_ref[:, j * HEAD_DIM:(j + 1) * HEAD_DIM] = kh.astype(BF16)
        kf_ref[:, j * HEAD_DIM:(j + 1) * HEAD_DIM] = kh
        v_ref[:, 2 * j * HEAD_DIM:(2 * j + 1) * HEAD_DIM] = qkv[:, V0 + j * HEAD_DIM:V0 + (j + 1) * HEAD_DIM].astype(BF16)
        v_ref[:, (2 * j + 1) * HEAD_DIM:(2 * j + 2) * HEAD_DIM] = ones
    vf_ref[...] = qkv[:, V0:Z0]

    z_ref[...] = _silu(jnp.dot(h, w_ref[:, Z0:X0], preferred_element_type=F32)).astype(BF16)

    s_ref[...] = jnp.dot(h_ext, w_ref[:, X0:DT0], preferred_element_type=F32)
    xc = (cb_ref[...] + cw_ref[0:1, :] * s_ref[halo - 1:halo - 1 + tm, :]
          + cw_ref[1:2, :] * s_ref[halo:halo + tm, :]
          + cw_ref[2:3, :] * s_ref[halo + 1:halo + 1 + tm, :])
    xbc_ref[...] = _silu(xc).astype(BF16)

    dt_raw = jnp.dot(h, wdt_ref[...], preferred_element_type=F32) + dtb_ref[...]
    dt_ref[...] = jnp.maximum(dt_raw, 0.0) + jnp.log1p(jnp.exp(-jnp.abs(dt_raw)))

    ga_ref[...] = _sigmoid(jnp.dot(h, wg_ref[:, 0:D_MODEL], preferred_element_type=F32)).astype(BF16)
    gs_ref[...] = _sigmoid(jnp.dot(h, wg_ref[:, D_MODEL:2 * D_MODEL], preferred_element_type=F32)).astype(BF16)


def _in_call(layer, x, mods, tabs, rope_cos, rope_sin, q_norm, k_norm, conv_w, conv_b, dt_bias, w_main, w_dt,
             w_gates):
    n_tok = x.shape[0]
    tm = ROW_TILE
    halo = SUBLANES_F32
    n_tiles = n_tok // tm
    hb = tm // halo
    last_hb = n_tok // halo - 1

    def row(width):
        return pl.BlockSpec((tm, width), lambda i, *_: (i, 0))

    grid_spec = pltpu.PrefetchScalarGridSpec(
        num_scalar_prefetch=4,
        grid=(n_tiles,),
        in_specs=[
            row(D_MODEL),
            pl.BlockSpec((halo, D_MODEL), lambda i, *_: (jnp.maximum(i * hb - 1, 0), 0)),
            pl.BlockSpec((halo, D_MODEL), lambda i, *_: (jnp.minimum((i + 1) * hb, last_hb), 0)),
            _layer_block(mods, layer),
            pl.BlockSpec((tm, HEAD_DIM), lambda i, c, p, n, r: (r[i], 0)),
            pl.BlockSpec((tm, HEAD_DIM), lambda i, c, p, n, r: (r[i], 0)),
            _layer_block(q_norm, layer), _layer_block(k_norm, layer),
            _layer_block(conv_w, layer), _layer_block(conv_b, layer), _layer_block(dt_bias, layer),
            _layer_block(w_main, layer, resident=True),
            _layer_block(w_dt, layer, resident=True),
            _layer_block(w_gates, layer, resident=True),
        ],
        out_specs=[row(ATTN_W), row(KV_W), row(2 * KV_W), row(KV_W), row(KV_W), row(D_INNER),
                   row(CONV_CH), row(2 * LANES), row(D_MODEL), row(D_MODEL)],
        scratch_shapes=[pltpu.VMEM((tm + 2 * halo, CONV_CH), F32)],
    )
    sds = jax.ShapeDtypeStruct
    return pl.pallas_call(
        _in_kernel,
        grid_spec=grid_spec,
        out_shape=[sds((n_tok, ATTN_W), BF16), sds((n_tok, KV_W), BF16), sds((n_tok, 2 * KV_W), BF16),
                   sds((n_tok, KV_W), F32), sds((n_tok, KV_W), F32), sds((n_tok, D_INNER), BF16),
                   sds((n_tok, CONV_CH), BF16), sds((n_tok, 2 * LANES), F32),
                   sds((n_tok, D_MODEL), BF16), sds((n_tok, D_MODEL), BF16)],
        compiler_params=_params(1),
        name="in_proj",
    )(tabs["crow"], tabs["hprev"], tabs["hnext"], tabs["rblk"],
      x, x, x, mods, rope_cos, rope_sin, q_norm, k_norm, conv_w, conv_b, dt_bias, w_main, w_dt, w_gates)


def _attn_kernel(*refs, has_cache):
    if has_cache:
        q_ref, k_ref, v_ref, kc_ref, vc_ref, o_ref, kt_ref, vcx_ref = refs
        past = kc_ref.shape[0]
    else:
        q_ref, k_ref, v_ref, o_ref, kt_ref = refs
        past = 0
    rep = H_Q // H_KV

    @pl.when(pl.program_id(1) == 0)
    def _():
        if has_cache:
            kt_ref[:, 0:past] = kc_ref[...].T.astype(BF16)
            for g in range(H_KV):
                vcx_ref[:, 2 * g * HEAD_DIM:(2 * g + 1) * HEAD_DIM] = vc_ref[:, g * HEAD_DIM:(g + 1) * HEAD_DIM].astype(BF16)
                vcx_ref[:, (2 * g + 1) * HEAD_DIM:(2 * g + 2) * HEAD_DIM] = jnp.ones((past, HEAD_DIM), BF16)
        kt_ref[:, past:] = k_ref[...].astype(F32).T.astype(BF16)

    for j in range(H_Q):
        g = j // rep
        q = q_ref[:, j * HEAD_DIM:(j + 1) * HEAD_DIM]
        s = jnp.dot(q, kt_ref[g * HEAD_DIM:(g + 1) * HEAD_DIM, :], preferred_element_type=F32)
        p = jnp.exp2(s - jnp.max(s, axis=-1, keepdims=True)).astype(BF16)
        o = jnp.dot(p[:, past:], v_ref[:, 2 * g * HEAD_DIM:(2 * g + 2) * HEAD_DIM], preferred_element_type=F32)
        if has_cache:
            o = o + jnp.dot(p[:, 0:past], vcx_ref[:, 2 * g * HEAD_DIM:(2 * g + 2) * HEAD_DIM],
                            preferred_element_type=F32)
        o_ref[:, j * HEAD_DIM:(j + 1) * HEAD_DIM] = (o[:, 0:HEAD_DIM] / o[:, HEAD_DIM:2 * HEAD_DIM]).astype(BF16)


def _attn_ctx_call(q, k, v, batch, seq):
    n_tok = batch * seq
    return pl.pallas_call(
        functools.partial(_attn_kernel, has_cache=False),
        grid=(batch, 1),
        in_specs=[pl.BlockSpec((seq, ATTN_W), lambda b, t: (b, 0)),
                  pl.BlockSpec((seq, KV_W), lambda b, t: (b, 0)),
                  pl.BlockSpec((seq, 2 * KV_W), lambda b, t: (b, 0))],
        out_specs=pl.BlockSpec((seq, ATTN_W), lambda b, t: (b, 0)),
        out_shape=jax.ShapeDtypeStruct((n_tok, ATTN_W), BF16),
        scratch_shapes=[pltpu.VMEM((KV_W, seq), BF16)],
        compiler_params=_params(2),
        name="attn_context",
    )(q, k, v)


def _attn_lat_call(q, k, v, cache_k, cache_v, n_ctx_tok, dec_batch, dec_seq, layer):
    tq = ATTN_Q_TILE
    t0 = n_ctx_tok // tq
    s0 = n_ctx_tok // dec_seq
    tiles = dec_seq // tq
    past = cache_k.shape[2]
    return pl.pallas_call(
        functools.partial(_attn_kernel, has_cache=True),
        grid=(dec_batch, tiles),
        in_specs=[pl.BlockSpec((tq, ATTN_W), lambda b, t: (t0 + b * tiles + t, 0)),
                  pl.BlockSpec((dec_seq, KV_W), lambda b, t: (s0 + b, 0)),
                  pl.BlockSpec((dec_seq, 2 * KV_W), lambda b, t: (s0 + b, 0)),
                  pl.BlockSpec((None, None, past, KV_W), lambda b, t: (b, layer, 0, 0)),
                  pl.BlockSpec((None, None, past, KV_W), lambda b, t: (b, layer, 0, 0))],
        out_specs=pl.BlockSpec((tq, ATTN_W), lambda b, t: (b * tiles + t, 0)),
        out_shape=jax.ShapeDtypeStruct((dec_batch * dec_seq, ATTN_W), BF16),
        scratch_shapes=[pltpu.VMEM((KV_W, past + dec_seq), BF16),
                        pltpu.VMEM((past, 2 * KV_W), BF16)],
        compiler_params=_params(2),
        name="attn_latent",
    )(q, k, v, cache_k, cache_v)


def _bf16_split3(x):
    c1 = x.astype(BF16).astype(F32)
    r = x - c1
    c2 = r.astype(BF16).astype(F32)
    c3 = (r - c2).astype(BF16).astype(F32)
    return c1, c2, c3


def _ssd_direction(x_ref, dt_ref, alog, h_ref, y_ref, ecol_ref, ew_ref, d, r0):
    q = SSM_CHUNK
    heads_per_group = SSM_HEADS // SSM_GROUPS
    dt = dt_ref[r0:r0 + q, :]
    a = -jnp.exp(alog)
    row = lax.broadcasted_iota(jnp.int32, (q, q), 0)
    col = lax.broadcasted_iota(jnp.int32, (q, q), 1)
    mask = (row >= col) if d == 0 else (row <= col)
    ones_tri = jnp.where(mask, 1.0, 0.0).astype(BF16)
    lane = lax.broadcasted_iota(jnp.int32, (q, LANES), 1)

    def pick_split(parts):
        return jnp.where(lane < SSM_HEADS, parts[0],
                         jnp.where(lane < 2 * SSM_HEADS, parts[1],
                                   jnp.where(lane < 3 * SSM_HEADS, parts[2], 0.0))).astype(BF16)

    da = jnp.concatenate(_bf16_split3(dt * a), axis=1).astype(BF16)
    cs = jnp.dot(ones_tri, da, preferred_element_type=F32)
    cum = (cs[:, 0:LANES] + cs[:, LANES:2 * LANES] + cs[:, 2 * LANES:3 * LANES]) * LOG2E
    cum_t = cum.T
    dt_t = dt.T
    total = cum[q - 1:q, :] if d == 0 else cum[0:1, :]
    state_w = dt * jnp.exp2(total - cum)
    cum_split = pick_split(_bf16_split3(cum))
    sw_split = pick_split(_bf16_split3(state_w))
    lo = lane < SSM_HEAD_DIM
    lo_row = lo[0:1, :]
    last = q - 1 if d == 0 else 0
    row2 = lax.broadcasted_iota(jnp.int32, (2 * q, LANES), 0)
    lane2 = lax.broadcasted_iota(jnp.int32, (2 * q, LANES), 1)
    pair_mask = jnp.where((row2 < q) == (lane2 < SSM_HEAD_DIM), 1.0, 0.0).astype(BF16)

    for g in range(SSM_GROUPS):
        b0 = D_INNER + g * SSM_STATE
        c0 = D_INNER + SSM_GROUPS * SSM_STATE + g * SSM_STATE
        bg_t = x_ref[r0:r0 + q, b0:b0 + SSM_STATE].astype(F32).T.astype(BF16)
        cg = x_ref[r0:r0 + q, c0:c0 + SSM_STATE]
        cb = jnp.dot(cg, bg_t, preferred_element_type=F32)
        h_t = h_ref[d, g]
        col_g = jnp.dot(cum_split, ecol_ref[:, g * heads_per_group * LANES:(g + 1) * heads_per_group * LANES],
                        preferred_element_type=F32)
        sw_g = jnp.dot(sw_split, ew_ref[:, g * GROUP_W:(g + 1) * GROUP_W], preferred_element_type=F32)
        y_off = jnp.dot(cg, h_t.astype(BF16), preferred_element_type=F32)
        xw_parts = []
        dec_parts = []
        for pp in range(heads_per_group // 2):
            h0 = g * heads_per_group + 2 * pp
            h1 = h0 + 1
            col0 = col_g[:, 2 * pp * LANES:(2 * pp + 1) * LANES]
            col1 = col_g[:, (2 * pp + 1) * LANES:(2 * pp + 2) * LANES]
            w0 = cb * jnp.exp2(jnp.where(mask, col0 - cum_t[h0:h0 + 1, :], -jnp.inf)) * dt_t[h0:h0 + 1, :]
            w1 = cb * jnp.exp2(jnp.where(mask, col1 - cum_t[h1:h1 + 1, :], -jnp.inf)) * dt_t[h1:h1 + 1, :]
            x0 = h0 * SSM_HEAD_DIM
            xpair = x_ref[r0:r0 + q, x0:x0 + LANES]
            w01 = jnp.concatenate([w0.astype(BF16), w1.astype(BF16)], axis=1)
            x01 = jnp.concatenate([xpair, xpair], axis=0) * pair_mask
            y_diag = jnp.dot(w01, x01, preferred_element_type=F32)
            colsel = jnp.where(lo, col0, col1)
            y_pair = y_diag + y_off[:, pp * LANES:(pp + 1) * LANES] * jnp.exp2(colsel)
            y_ref[r0:r0 + q, x0:x0 + LANES] = y_pair.astype(BF16)
            xw_parts.append((xpair.astype(F32) * sw_g[:, pp * LANES:(pp + 1) * LANES]).astype(BF16))
            dec_parts.append(jnp.exp2(jnp.where(lo_row, col0[last:last + 1, :], col1[last:last + 1, :])))
        xw = jnp.concatenate(xw_parts, axis=1)
        decay = jnp.concatenate(dec_parts, axis=1)
        h_ref[d, g] = h_t * decay + jnp.dot(bg_t, xw, preferred_element_type=F32)


def _ssd_kernel(chf_ref, chb_ref, first_ref, last_ref, islat_ref, latb_ref, ctxb_ref,
                xf_ref, xb_ref, dtf_ref, dtb_ref, alog_ref, init_ref, ecol_ref, ew_ref,
                yf_ref, yb_ref, fin_ref, h_ref):
    del chf_ref, chb_ref, latb_ref, ctxb_ref
    step = pl.program_id(0)

    @pl.when((first_ref[step] == 1) & (islat_ref[step] == 0))
    def _():
        h_ref[...] = jnp.zeros(h_ref.shape, F32)

    @pl.when((first_ref[step] == 1) & (islat_ref[step] == 1))
    def _():
        for d in range(2):
            for g in range(SSM_GROUPS):
                h_ref[d, g] = init_ref[d, g * GROUP_W:(g + 1) * GROUP_W, :].T

    for sub in range(SSD_CHUNKS_PER_STEP):
        _ssd_direction(xf_ref, dtf_ref, alog_ref[0], h_ref, yf_ref, ecol_ref, ew_ref, 0, sub * SSM_CHUNK)
        _ssd_direction(xb_ref, dtb_ref, alog_ref[1], h_ref, yb_ref, ecol_ref, ew_ref, 1,
                       (SSD_CHUNKS_PER_STEP - 1 - sub) * SSM_CHUNK)

    @pl.when((last_ref[step] == 1) & (islat_ref[step] == 0))
    def _():
        for d in range(2):
            for g in range(SSM_GROUPS):
                fin_ref[d, g * GROUP_W:(g + 1) * GROUP_W, :] = h_ref[d, g].T


def _expansion_matrices():
    r = np.arange(LANES)
    head = r % SSM_HEADS
    live = r < DT_COPIES * SSM_HEADS
    ecol = (live[:, None] & (head[:, None] == (np.arange(SSM_HEADS * LANES) // LANES)[None, :]))
    ew = (live[:, None] & (head[:, None] == (np.arange(D_INNER) // SSM_HEAD_DIM)[None, :]))
    return jnp.asarray(ecol, BF16), jnp.asarray(ew, BF16)


def _ssd_call(layer, xbc, dt, a_log, state_ssd, tabs, new_s):
    n_tok = xbc.shape[0]
    q = SSD_CHUNKS_PER_STEP * SSM_CHUNK
    n_steps = n_tok // q
    hpn = SSM_HEADS * SSM_HEAD_DIM
    dec_batch, depth = state_ssd.shape[0], state_ssd.shape[1]
    batch = int(tabs["batch"])
    init = state_ssd.reshape(dec_batch, depth, 2, hpn, SSM_STATE)
    ecol, ew = _expansion_matrices()

    in_specs = [
        pl.BlockSpec((q, CONV_CH), lambda j, cf, cb, *_: (cf[j], 0)),
        pl.BlockSpec((q, CONV_CH), lambda j, cf, cb, *_: (cb[j], 0)),
        pl.BlockSpec((q, LANES), lambda j, cf, cb, *_: (cf[j], 0)),
        pl.BlockSpec((q, LANES), lambda j, cf, cb, *_: (cb[j], 1)),
        _layer_block(a_log, layer),
        pl.BlockSpec((None, None, 2, hpn, SSM_STATE),
                     lambda j, cf, cb, fi, la, il, lb, xb: (lb[j], layer, 0, 0, 0)),
        _resident(ecol.shape, lambda j, *_: (0, 0)),
        _resident(ew.shape, lambda j, *_: (0, 0)),
    ]
    in_specs.append(pl.BlockSpec(memory_space=pl.ANY))
    args = [xbc, xbc, dt, dt, a_log, init, ecol, ew, new_s]
    n_prefetch = 7
    grid_spec = pltpu.PrefetchScalarGridSpec(
        num_scalar_prefetch=n_prefetch,
        grid=(n_steps,),
        in_specs=in_specs,
        out_specs=[
            pl.BlockSpec((q, D_INNER), lambda j, cf, cb, *_: (cf[j], 0)),
            pl.BlockSpec((q, D_INNER), lambda j, cf, cb, *_: (cb[j], 0)),
            pl.BlockSpec((None, None, 2, hpn, SSM_STATE),
                         lambda j, cf, cb, fi, la, il, lb, xb: (xb[j], layer, 0, 0, 0)),
        ],
        scratch_shapes=[pltpu.VMEM((2, SSM_GROUPS, SSM_STATE, GROUP_W), F32)],
    )

    alias_in = n_prefetch + len(args) - 1

    def body(*refs):
        _ssd_kernel(*(refs[:alias_in] + refs[alias_in + 1:]))

    return pl.pallas_call(
        body,
        grid_spec=grid_spec,
        out_shape=[jax.ShapeDtypeStruct((n_tok, D_INNER), BF16),
                   jax.ShapeDtypeStruct((n_tok, D_INNER), BF16),
                   jax.ShapeDtypeStruct(new_s.shape, F32)],
        input_output_aliases={alias_in: 2},
        compiler_params=_params(1),
        name="ssd_scan",
    )(tabs["s_chunk_f"], tabs["s_chunk_b"], tabs["s_first"], tabs["s_last"], tabs["s_islat"],
      tabs["s_latb"], tabs["s_ctxb"], *args)


COMB_SPLITS = 1


def _comb_kernel(crow_ref, attn_c_ref, attn_l_ref, yf_ref, yb_ref, xs_ref, z_ref, ga_ref, gs_ref, x_ref, mod_ref,
                 dskip_ref, nw_ref, wa_ref, ws_ref, wo_ref, x1_ref, h2_ref, *, n_ctx_tiles):
    i = pl.program_id(0)
    mod = mod_ref[pl.ds(crow_ref[i], 1), :]
    gate1 = mod[:, 2 * D_MODEL:3 * D_MODEL]
    shift2 = mod[:, 3 * D_MODEL:4 * D_MODEL]
    scale2 = mod[:, 4 * D_MODEL:5 * D_MODEL]
    is_ctx = i < n_ctx_tiles
    rows = x_ref.shape[0] // COMB_SPLITS
    for r in range(COMB_SPLITS):
        sl = slice(r * rows, (r + 1) * rows)
        attn = jnp.where(is_ctx, attn_c_ref[sl, :], attn_l_ref[sl, :])
        y = yf_ref[sl, :].astype(F32) + yb_ref[sl, :].astype(F32) + dskip_ref[...] * xs_ref[sl, :].astype(F32)
        y = y * z_ref[sl, :].astype(F32)
        yn = (y * lax.rsqrt(jnp.mean(y * y, axis=-1, keepdims=True) + EPS)) * nw_ref[...]
        merged = (ga_ref[sl, :].astype(F32) * jnp.dot(attn, wa_ref[...], preferred_element_type=F32)
                  + gs_ref[sl, :].astype(F32) * jnp.dot(yn.astype(BF16), ws_ref[...], preferred_element_type=F32))
        x1 = x_ref[sl, :] + gate1 * jnp.dot(merged.astype(BF16), wo_ref[...], preferred_element_type=F32)
        x1_ref[sl, :] = x1
        h2_ref[sl, :] = _modulate(x1, shift2, scale2).astype(BF16)


def _comb_call(layer, tabs, attn_c, attn_l, yf, yb, xbc, z, ga, gs, x, mods, d_skip, ssd_norm, w_attn_o, w_ssd_o,
               w_out):
    n_tok = x.shape[0]
    tm = ROW_TILE
    nct = attn_c.shape[0] // tm

    def row(width):
        return pl.BlockSpec((tm, width), lambda i, *_: (i, 0))

    grid_spec = pltpu.PrefetchScalarGridSpec(
        num_scalar_prefetch=1,
        grid=(n_tok // tm,),
        in_specs=[pl.BlockSpec((tm, ATTN_W), lambda i, *_: (jnp.minimum(i, nct - 1), 0)),
                  pl.BlockSpec((tm, ATTN_W), lambda i, *_: (jnp.maximum(i - nct, 0), 0)),
                  row(D_INNER), row(D_INNER), row(D_INNER), row(D_INNER),
                  row(D_MODEL), row(D_MODEL), row(D_MODEL),
                  _layer_block(mods, layer), _layer_block(d_skip, layer), _layer_block(ssd_norm, layer),
                  _layer_block(w_attn_o, layer, resident=True),
                  _layer_block(w_ssd_o, layer, resident=True),
                  _layer_block(w_out, layer, resident=True)],
        out_specs=[row(D_MODEL), row(D_MODEL)],
    )
    return pl.pallas_call(
        functools.partial(_comb_kernel, n_ctx_tiles=nct),
        grid_spec=grid_spec,
        out_shape=[jax.ShapeDtypeStruct((n_tok, D_MODEL), F32),
                   jax.ShapeDtypeStruct((n_tok, D_MODEL), BF16)],
        compiler_params=_params(1),
        name="combine",
    )(tabs["crow"], attn_c, attn_l, yf, yb, xbc, z, ga, gs, x, mods, d_skip, ssd_norm, w_attn_o, w_ssd_o, w_out)


FFN_COL_CHUNK = D_FF // 2


def _ffn_kernel(crow_ref, hprev_ref, hnext_ref,
                h_ref, hp_ref, hn_ref, x_ref, mod_ref, cw_ref, cb_ref, wu_ref, wd_ref,
                o_ref, sv_ref, sg_ref):
    i = pl.program_id(0)
    tm = h_ref.shape[0]
    halo = hp_ref.shape[0]
    mod = mod_ref[pl.ds(crow_ref[i], 1), :]
    gate2 = mod[:, 5 * D_MODEL:6 * D_MODEL]
    hp = (hp_ref[...].astype(F32) * hprev_ref[i].astype(F32)).astype(BF16)
    hn = (hn_ref[...].astype(F32) * hnext_ref[i].astype(F32)).astype(BF16)
    h_ext = jnp.concatenate([hp, h_ref[...], hn], axis=0)
    acc = jnp.zeros((tm, D_MODEL), F32)
    cc = FFN_COL_CHUNK
    for c in range(D_FF // cc):
        halves = []
        for s_ref, base in ((sv_ref, c * cc), (sg_ref, D_FF + c * cc)):
            s_ref[...] = jnp.dot(h_ext, wu_ref[:, base:base + cc], preferred_element_type=F32)
            u = (cb_ref[:, base:base + cc]
                 + cw_ref[0:1, base:base + cc] * s_ref[halo - 1:halo - 1 + tm, :]
                 + cw_ref[1:2, base:base + cc] * s_ref[halo:halo + tm, :]
                 + cw_ref[2:3, base:base + cc] * s_ref[halo + 1:halo + 1 + tm, :])
            halves.append(u)
        act = (_silu(halves[1]) * halves[0]).astype(BF16)
        acc = acc + jnp.dot(act, wd_ref[c * cc:(c + 1) * cc, :], preferred_element_type=F32)
    o_ref[...] = x_ref[...] + gate2 * acc


def _ffn_call(layer, tabs, h2, x1, mods, ffn_conv_w, ffn_conv_b, w_up, w_down):
    n_tok = x1.shape[0]
    tm = ROW_TILE
    halo = SUBLANES_BF16
    hb = tm // halo
    last_hb = n_tok // halo - 1

    grid_spec = pltpu.PrefetchScalarGridSpec(
        num_scalar_prefetch=3,
        grid=(n_tok // tm,),
        in_specs=[pl.BlockSpec((tm, D_MODEL), lambda i, *_: (i, 0)),
                  pl.BlockSpec((halo, D_MODEL), lambda i, *_: (jnp.maximum(i * hb - 1, 0), 0)),
                  pl.BlockSpec((halo, D_MODEL), lambda i, *_: (jnp.minimum((i + 1) * hb, last_hb), 0)),
                  pl.BlockSpec((tm, D_MODEL), lambda i, *_: (i, 0)),
                  _layer_block(mods, layer), _layer_block(ffn_conv_w, layer), _layer_block(ffn_conv_b, layer),
                  _layer_block(w_up, layer, resident=True),
                  _layer_block(w_down, layer, resident=True)],
        out_specs=pl.BlockSpec((tm, D_MODEL), lambda i, *_: (i, 0)),
        scratch_shapes=[pltpu.VMEM((tm + 2 * halo, FFN_COL_CHUNK), F32),
                        pltpu.VMEM((tm + 2 * halo, FFN_COL_CHUNK), F32)],
    )
    return pl.pallas_call(
        _ffn_kernel,
        grid_spec=grid_spec,
        out_shape=jax.ShapeDtypeStruct((n_tok, D_MODEL), F32),
        compiler_params=_params(1),
        name="conv_ffn",
    )(tabs["crow"], tabs["hprev"], tabs["hnext"], h2, h2, h2, x1, mods, ffn_conv_w, ffn_conv_b, w_up, w_down)


def _tables(batch, seq, dec_batch, dec_seq):
    tm = ROW_TILE
    crow, hprev, hnext, rblk = [], [], [], []
    for b in range(batch):
        for t in range(seq // tm):
            crow.append(0)
            hprev.append(int(t > 0))
            hnext.append(int(t < seq // tm - 1))
            rblk.append(0)
    for b in range(dec_batch):
        for t in range(dec_seq // tm):
            crow.append(1 + b)
            hprev.append(int(t > 0))
            hnext.append(int(t < dec_seq // tm - 1))
            rblk.append(1 + t)
    q = SSD_CHUNKS_PER_STEP * SSM_CHUNK
    assert seq % q == 0 and dec_seq % q == 0
    seqs = [(False, b, seq // q) for b in range(batch)] + [(True, b, dec_seq // q) for b in range(dec_batch)]
    s_chunk_f, s_chunk_b, s_first, s_last, s_islat, s_latb, s_ctxb = [], [], [], [], [], [], []
    base = 0
    for is_lat, b, n in seqs:
        for c in range(n):
            s_chunk_f.append(base + c)
            s_chunk_b.append(base + n - 1 - c)
            s_first.append(int(c == 0))
            s_last.append(int(c == n - 1))
            s_islat.append(int(is_lat))
            s_latb.append(b if is_lat else 0)
            s_ctxb.append(batch - 1 if is_lat else b)
        base += n
    tabs = dict(crow=crow, hprev=hprev, hnext=hnext, rblk=rblk, s_chunk_f=s_chunk_f, s_chunk_b=s_chunk_b,
                s_first=s_first, s_last=s_last, s_islat=s_islat, s_latb=s_latb, s_ctxb=s_ctxb)
    tabs = {k: jnp.asarray(np.asarray(v, np.int32)) for k, v in tabs.items()}
    tabs["batch"] = batch
    return tabs


def _rope_tables(dec_seq):
    rows = dec_seq // GRID_W
    row = jnp.repeat(jnp.arange(rows), GRID_W).astype(F32)
    col = jnp.tile(jnp.arange(GRID_W), rows).astype(F32)
    half = HEAD_DIM // 2
    inv_freq = ROPE_THETA ** (-jnp.arange(0, half, 2, dtype=F32) / half)
    ang_r = row[:, None] * inv_freq
    ang_c = col[:, None] * inv_freq
    ang = jnp.concatenate([ang_r, ang_r, ang_c, ang_c], axis=-1)
    sign = jnp.where((jnp.arange(HEAD_DIM) & (HEAD_DIM // 4)) == 0, -1.0, 1.0).astype(F32)
    cos = jnp.concatenate([jnp.ones((ROW_TILE, HEAD_DIM), F32), jnp.cos(ang)], axis=0)
    sin = jnp.concatenate([jnp.zeros((ROW_TILE, HEAD_DIM), F32), jnp.sin(ang) * sign], axis=0)
    return cos, sin


def _dir_lanes(p):
    tail = jnp.zeros(p.shape[:-1] + (LANES - DT_COPIES * SSM_HEADS,), p.dtype)
    return jnp.concatenate([p] * DT_COPIES + [tail], axis=-1)


def _cast_kernel(w_ref, o_ref):
    o_ref[...] = w_ref[...].astype(BF16)


def _cast_call(w, n_cols, tn):
    depth, k, _ = w.shape
    assert n_cols % tn == 0 and tn % LANES == 0
    return pl.pallas_call(
        _cast_kernel,
        grid=(depth, n_cols // tn),
        in_specs=[pl.BlockSpec((1, k, tn), lambda l, j: (l, 0, j))],
        out_specs=pl.BlockSpec((1, k, tn), lambda l, j: (l, 0, j)),
        out_shape=jax.ShapeDtypeStruct((depth, k, n_cols), BF16),
        compiler_params=_params(2),
        name="cast_bf16",
    )(w)


def _split_w_in(w_in):
    depth = w_in.shape[0]
    w_dt = w_in[:, :, DT0:DT0 + 2 * SSM_HEADS].reshape(depth, D_MODEL, 2, SSM_HEADS)
    w_dt = _dir_lanes(w_dt).reshape(depth, D_MODEL, 2 * LANES)
    return (_cast_call(w_in, DT0, DT0 // 4), w_dt.astype(BF16), w_in[:, :, DT0 + 2 * SSM_HEADS:].astype(BF16))


def kernel(x_prompt, x_sample, c, cache_k, cache_v, state_ssd, c_ctx, w_mod, b_mod, w_in, q_norm, k_norm,
           conv_w, conv_b, dt_bias, a_log, d_skip, ssd_norm, w_attn_o, w_ssd_o, w_out, w_up,
           ffn_conv_w, ffn_conv_b, w_down):
    batch, seq, _ = x_prompt.shape
    dec_batch, dec_seq, _ = x_sample.shape
    depth = w_in.shape[0]
    past = cache_k.shape[2]
    n_ctx = batch * seq
    assert seq % ROW_TILE == 0 and dec_seq % ROW_TILE == 0 and n_ctx % dec_seq == 0
    assert 1 + dec_batch <= SUBLANES_F32

    tabs = _tables(batch, seq, dec_batch, dec_seq)
    rope_cos, rope_sin = _rope_tables(dec_seq)
    cond = jnp.concatenate([c_ctx[None, :], c, jnp.zeros((SUBLANES_F32 - 1 - dec_batch, D_MODEL), F32)], axis=0)
    mods = _mod_call(cond, w_mod, b_mod)

    w_main, w_dt, w_gates = _split_w_in(w_in)
    wa = _cast_call(w_attn_o, D_MODEL, D_MODEL)
    ws = _cast_call(w_ssd_o, D_MODEL, D_MODEL // 2)
    wo = _cast_call(w_out, D_MODEL, D_MODEL)
    wu = _cast_call(w_up, 2 * D_FF, D_FF // 2)
    wd = _cast_call(w_down, D_MODEL, D_MODEL // 2)
    qn = q_norm.reshape(depth, 1, HEAD_DIM)
    kn = k_norm.reshape(depth, 1, HEAD_DIM)
    cvb = conv_b.reshape(depth, 1, CONV_CH)
    fcb = ffn_conv_b.reshape(depth, 1, 2 * D_FF)
    dtb = _dir_lanes(dt_bias).reshape(depth, 1, 2 * LANES)
    alog = _dir_lanes(a_log).reshape(depth, 2, 1, LANES)
    dsk = jnp.repeat(d_skip, SSM_HEAD_DIM, axis=-1).reshape(depth, 1, D_INNER)
    nrm = ssd_norm.reshape(depth, 1, D_INNER)
    ck = cache_k.reshape(dec_batch, depth, past, KV_W)
    cv = cache_v.reshape(dec_batch, depth, past, KV_W)

    x = jnp.concatenate([x_prompt.reshape(n_ctx, D_MODEL), x_sample.reshape(dec_batch * dec_seq, D_MODEL)], axis=0)
    new_k, new_v = [], []
    new_s = jnp.zeros((batch, depth, 2, SSM_HEADS * SSM_HEAD_DIM, SSM_STATE), F32)
    for l in range(depth):
        q, k, v, kf, vf, z, xbc, dt, ga, gs = _in_call(
            l, x, mods, tabs, rope_cos, rope_sin, qn, kn, conv_w, cvb, dtb, w_main, w_dt, w_gates)
        attn_c = _attn_ctx_call(q, k, v, batch, seq)
        attn_l = _attn_lat_call(q, k, v, ck, cv, n_ctx, dec_batch, dec_seq, l)
        yf, yb, new_s = _ssd_call(l, xbc, dt, alog, state_ssd, tabs, new_s)
        x1, h2 = _comb_call(l, tabs, attn_c, attn_l, yf, yb, xbc, z, ga, gs, x, mods, dsk, nrm, wa, ws, wo)
        x = _ffn_call(l, tabs, h2, x1, mods, ffn_conv_w, fcb, wu, wd)
        new_k.append(kf[:n_ctx].reshape(batch, seq, H_KV, HEAD_DIM))
        new_v.append(vf[:n_ctx].reshape(batch, seq, H_KV, HEAD_DIM))
    y_prompt = x[:n_ctx].reshape(batch, seq, D_MODEL)
    y_sample = x[n_ctx:].reshape(dec_batch, dec_seq, D_MODEL)
    new_state = new_s.reshape(batch, depth, 2, SSM_HEADS, SSM_HEAD_DIM, SSM_STATE)
    return (y_prompt, y_sample, jnp.stack(new_k, axis=1), jnp.stack(new_v, axis=1), new_state)
```

```python
import functools
import math

import numpy as np
import jax
import jax.numpy as jnp
from jax import lax
from jax.experimental import pallas as pl
from jax.experimental.pallas import tpu as pltpu

F32 = jnp.float32
BF16 = jnp.bfloat16

D_MODEL = 1024
GRID_W = 64
EPS = 1e-6
H_Q = 8
H_KV = 2
HEAD_DIM = 128
ATTN_W = H_Q * HEAD_DIM
KV_W = H_KV * HEAD_DIM
ROPE_THETA = 10000.0
D_INNER = 2 * D_MODEL
SSM_HEAD_DIM = 64
SSM_HEADS = D_INNER // SSM_HEAD_DIM
SSM_GROUPS = 4
SSM_STATE = 128
SSM_CHUNK = 128
CONV_CH = D_INNER + 2 * SSM_GROUPS * SSM_STATE
D_FF = 2816
GROUP_W = (SSM_HEADS // SSM_GROUPS) * SSM_HEAD_DIM
LOG2E = 1.4426950408889634

LANES = 128
SUBLANES_F32 = 8
SUBLANES_BF16 = 16
VMEM_LIMIT_BYTES = 56 * 1024 * 1024

Q0 = 0
K0 = Q0 + ATTN_W
V0 = K0 + KV_W
Z0 = V0 + KV_W
X0 = Z0 + D_INNER
DT0 = X0 + CONV_CH
DT_COPIES = 3

ROW_TILE = 256
SSD_CHUNKS_PER_STEP = 2
ATTN_Q_TILE = 512


def _sigmoid(x):
    return 0.5 * jnp.tanh(0.5 * x) + 0.5


def _silu(x):
    h = 0.5 * x
    return h * jnp.tanh(h) + h


def _params(n_axes):
    return pltpu.CompilerParams(dimension_semantics=("arbitrary",) * n_axes,
                                vmem_limit_bytes=VMEM_LIMIT_BYTES)


def _resident(shape, index_map):
    return pl.BlockSpec(shape, index_map, pipeline_mode=pl.Buffered(1))


def _layer_block(arr, layer, resident=False):
    shape = (None,) + tuple(arr.shape[1:])
    nz = len(arr.shape) - 1

    def index_map(*_):
        return (layer,) + (0,) * nz

    return _resident(shape, index_map) if resident else pl.BlockSpec(shape, index_map)


def _mod_kernel(cond_ref, w_ref, b_ref, o_ref):
    a = _silu(cond_ref[...]).astype(BF16)
    o_ref[0] = jnp.dot(a, w_ref[0].astype(BF16), preferred_element_type=F32) + b_ref[0]


def _mod_call(cond, w_mod, b_mod):
    depth, _, n = w_mod.shape
    tn = 1536
    return pl.pallas_call(
        _mod_kernel,
        grid=(depth, n // tn),
        in_specs=[pl.BlockSpec(cond.shape, lambda l, j: (0, 0)),
                  pl.BlockSpec((1, D_MODEL, tn), lambda l, j: (l, 0, j)),
                  pl.BlockSpec((1, 1, tn), lambda l, j: (l, 0, j))],
        out_specs=pl.BlockSpec((1, cond.shape[0], tn), lambda l, j: (l, 0, j)),
        out_shape=jax.ShapeDtypeStruct((depth, cond.shape[0], n), F32),
        compiler_params=_params(2),
        name="mod_vectors",
    )(cond, w_mod, b_mod.reshape(depth, 1, n))


def _modulate(x, shift, scale):
    ms = jnp.mean(x * x, axis=-1, keepdims=True)
    return (x * lax.rsqrt(ms + EPS)) * (1.0 + scale) + shift


def _in_kernel(crow_ref, hprev_ref, hnext_ref, rblk_ref,
               x_ref, xp_ref, xn_ref, mod_ref, cos_ref, sin_ref, qn_ref, kn_ref, cw_ref, cb_ref,
               dtb_ref, w_ref, wdt_ref, wg_ref,
               q_ref, k_ref, v_ref, kf_ref, vf_ref, z_ref, xbc_ref, dt_ref, ga_ref, gs_ref,
               s_ref):
    del rblk_ref
    i = pl.program_id(0)
    tm = x_ref.shape[0]
    halo = xp_ref.shape[0]
    mod = mod_ref[pl.ds(crow_ref[i], 1), :]
    shift = mod[:, 0:D_MODEL]
    scale = mod[:, D_MODEL:2 * D_MODEL]
    hm = _modulate(x_ref[...], shift, scale)
    hp = _modulate(xp_ref[...], shift, scale) * hprev_ref[i].astype(F32)
    hn = _modulate(xn_ref[...], shift, scale) * hnext_ref[i].astype(F32)
    h = hm.astype(BF16)
    h_ext = jnp.concatenate([hp, hm, hn], axis=0).astype(BF16)

    qkv = jnp.dot(h, w_ref[:, Q0:Z0], preferred_element_type=F32)
    cos = cos_ref[...]
    sin = sin_ref[...]
    lane = lax.broadcasted_iota(jnp.int32, (tm, HEAD_DIM), 1)
    first_quarter = (lane & (HEAD_DIM // 4)) == 0

    def norm_rope(u, w):
        un = u * lax.rsqrt(jnp.mean(u * u, axis=-1, keepdims=True) + EPS) * w
        rot = jnp.where(first_quarter, pltpu.roll(un, HEAD_DIM - HEAD_DIM // 4, 1),
                        pltpu.roll(un, HEAD_DIM // 4, 1))
        return un * cos + rot * sin

    q_scale = HEAD_DIM ** -0.5 * LOG2E
    for j in range(H_Q):
        qh = norm_rope(qkv[:, j * HEAD_DIM:(j + 1) * HEAD_DIM], qn_ref[...])
        q_ref[:, j * HEAD_DIM:(j + 1) * HEAD_DIM] = (qh * q_scale).astype(BF16)
    ones = jnp.ones((tm, HEAD_DIM), BF16)
    for j in range(H_KV):
        kh = norm_rope(qkv[:, K0 + j * HEAD_DIM:K0 + (j + 1) * HEAD_DIM], kn_ref[...])
        k_ref[:, j * HEAD_DIM:(j + 1) * HEAD_DIM] = kh.astype(BF16)
        kf_ref[:, j * HEAD_DIM:(j + 1) * HEAD_DIM] = kh
        v_ref[:, 2 * j * HEAD_DIM:(2 * j + 1) * HEAD_DIM] = qkv[:, V0 + j * HEAD_DIM:V0 + (j + 1) * HEAD_DIM].astype(BF16)
        v_ref[:, (2 * j + 1) * HEAD_DIM:(2 * j + 2) * HEAD_DIM] = ones
    vf_ref[...] = qkv[:, V0:Z0]

    z_ref[...] = _silu(jnp.dot(h, w_ref[:, Z0:X0], preferred_element_type=F32)).astype(BF16)

    s_ref[...] = jnp.dot(h_ext, w_ref[:, X0:DT0], preferred_element_type=F32)
    xc = (cb_ref[...] + cw_ref[0:1, :] * s_ref[halo - 1:halo - 1 + tm, :]
          + cw_ref[1:2, :] * s_ref[halo:halo + tm, :]
          + cw_ref[2:3, :] * s_ref[halo + 1:halo + 1 + tm, :])
    xbc_ref[...] = _silu(xc).astype(BF16)

    dt_raw = jnp.dot(h, wdt_ref[...], preferred_element_type=F32) + dtb_ref[...]
    dt_ref[...] = jnp.maximum(dt_raw, 0.0) + jnp.log1p(jnp.exp(-jnp.abs(dt_raw)))

    ga_ref[...] = _sigmoid(jnp.dot(h, wg_ref[:, 0:D_MODEL], preferred_element_type=F32)).astype(BF16)
    gs_ref[...] = _sigmoid(jnp.dot(h, wg_ref[:, D_MODEL:2 * D_MODEL], preferred_element_type=F32)).astype(BF16)


def _in_call(layer, x, mods, tabs, rope_cos, rope_sin, q_norm, k_norm, conv_w, conv_b, dt_bias, w_main, w_dt,
             w_gates):
    n_tok = x.shape[0]
    tm = ROW_TILE
    halo = SUBLANES_F32
    n_tiles = n_tok // tm
    hb = tm // halo
    last_hb = n_tok // halo - 1

    def row(width):
        return pl.BlockSpec((tm, width), lambda i, *_: (i, 0))

    grid_spec = pltpu.PrefetchScalarGridSpec(
        num_scalar_prefetch=4,
        grid=(n_tiles,),
        in_specs=[
            row(D_MODEL),
            pl.BlockSpec((halo, D_MODEL), lambda i, *_: (jnp.maximum(i * hb - 1, 0), 0)),
            pl.BlockSpec((halo, D_MODEL), lambda i, *_: (jnp.minimum((i + 1) * hb, last_hb), 0)),
            _layer_block(mods, layer),
            pl.BlockSpec((tm, HEAD_DIM), lambda i, c, p, n, r: (r[i], 0)),
            pl.BlockSpec((tm, HEAD_DIM), lambda i, c, p, n, r: (r[i], 0)),
            _layer_block(q_norm, layer), _layer_block(k_norm, layer),
            _layer_block(conv_w, layer), _layer_block(conv_b, layer), _layer_block(dt_bias, layer),
            _layer_block(w_main, layer, resident=True),
            _layer_block(w_dt, layer, resident=True),
            _layer_block(w_gates, layer, resident=True),
        ],
        out_specs=[row(ATTN_W), row(KV_W), row(2 * KV_W), row(KV_W), row(KV_W), row(D_INNER),
                   row(CONV_CH), row(2 * LANES), row(D_MODEL), row(D_MODEL)],
        scratch_shapes=[pltpu.VMEM((tm + 2 * halo, CONV_CH), F32)],
    )
    sds = jax.ShapeDtypeStruct
    return pl.pallas_call(
        _in_kernel,
        grid_spec=grid_spec,
        out_shape=[sds((n_tok, ATTN_W), BF16), sds((n_tok, KV_W), BF16), sds((n_tok, 2 * KV_W), BF16),
                   sds((n_tok, KV_W), F32), sds((n_tok, KV_W), F32), sds((n_tok, D_INNER), BF16),
                   sds((n_tok, CONV_CH), BF16), sds((n_tok, 2 * LANES), F32),
                   sds((n_tok, D_MODEL), BF16), sds((n_tok, D_MODEL), BF16)],
        compiler_params=_params(1),
        name="in_proj",
    )(tabs["crow"], tabs["hprev"], tabs["hnext"], tabs["rblk"],
      x, x, x, mods, rope_cos, rope_sin, q_norm, k_norm, conv_w, conv_b, dt_bias, w_main, w_dt, w_gates)


def _attn_kernel(*refs, has_cache):
    if has_cache:
        q_ref, k_ref, v_ref, kc_ref, vc_ref, o_ref, kt_ref, vcx_ref = refs
        past = kc_ref.shape[0]
    else:
        q_ref, k_ref, v_ref, kf_ref, vf_ref, _, _, o_ref, nk_ref, nv_ref, kt_ref = refs
        past = 0
        nk_ref[...] = kf_ref[...]
        nv_ref[...] = vf_ref[...]
    rep = H_Q // H_KV

    @pl.when(pl.program_id(1) == 0)
    def _():
        if has_cache:
            kt_ref[:, 0:past] = kc_ref[...].T.astype(BF16)
            for g in range(H_KV):
                vcx_ref[:, 2 * g * HEAD_DIM:(2 * g + 1) * HEAD_DIM] = vc_ref[:, g * HEAD_DIM:(g + 1) * HEAD_DIM].astype(BF16)
                vcx_ref[:, (2 * g + 1) * HEAD_DIM:(2 * g + 2) * HEAD_DIM] = jnp.ones((past, HEAD_DIM), BF16)
        kt_ref[:, past:] = k_ref[...].astype(F32).T.astype(BF16)

    for j in range(H_Q):
        g = j // rep
        q = q_ref[:, j * HEAD_DIM:(j + 1) * HEAD_DIM]
        s = jnp.dot(q, kt_ref[g * HEAD_DIM:(g + 1) * HEAD_DIM, :], preferred_element_type=F32)
        p = jnp.exp2(s - jnp.max(s, axis=-1, keepdims=True)).astype(BF16)
        o = jnp.dot(p[:, past:], v_ref[:, 2 * g * HEAD_DIM:(2 * g + 2) * HEAD_DIM], preferred_element_type=F32)
        if has_cache:
            o = o + jnp.dot(p[:, 0:past], vcx_ref[:, 2 * g * HEAD_DIM:(2 * g + 2) * HEAD_DIM],
                            preferred_element_type=F32)
        o_ref[:, j * HEAD_DIM:(j + 1) * HEAD_DIM] = (o[:, 0:HEAD_DIM] / o[:, HEAD_DIM:2 * HEAD_DIM]).astype(BF16)


def _attn_ctx_call(layer, q, k, v, kf, vf, new_k, new_v, batch, seq):
    n_tok = batch * seq
    stacked = pl.BlockSpec((None, None, seq, KV_W), lambda b, t: (b, layer, 0, 0))
    return pl.pallas_call(
        functools.partial(_attn_kernel, has_cache=False),
        grid=(batch, 1),
        in_specs=[pl.BlockSpec((seq, ATTN_W), lambda b, t: (b, 0)),
                  pl.BlockSpec((seq, KV_W), lambda b, t: (b, 0)),
                  pl.BlockSpec((seq, 2 * KV_W), lambda b, t: (b, 0)),
                  pl.BlockSpec((seq, KV_W), lambda b, t: (b, 0)),
                  pl.BlockSpec((seq, KV_W), lambda b, t: (b, 0)),
                  pl.BlockSpec(memory_space=pl.ANY),
                  pl.BlockSpec(memory_space=pl.ANY)],
        out_specs=[pl.BlockSpec((seq, ATTN_W), lambda b, t: (b, 0)), stacked, stacked],
        out_shape=[jax.ShapeDtypeStruct((n_tok, ATTN_W), BF16),
                   jax.ShapeDtypeStruct(new_k.shape, F32), jax.ShapeDtypeStruct(new_v.shape, F32)],
        scratch_shapes=[pltpu.VMEM((KV_W, seq), BF16)],
        input_output_aliases={5: 1, 6: 2},
        compiler_params=_params(2),
        name="attn_context",
    )(q, k, v, kf, vf, new_k, new_v)


def _attn_lat_call(q, k, v, cache_k, cache_v, n_ctx_tok, dec_batch, dec_seq, layer):
    tq = ATTN_Q_TILE
    t0 = n_ctx_tok // tq
    s0 = n_ctx_tok // dec_seq
    tiles = dec_seq // tq
    past = cache_k.shape[2]
    return pl.pallas_call(
        functools.partial(_attn_kernel, has_cache=True),
        grid=(dec_batch, tiles),
        in_specs=[pl.BlockSpec((tq, ATTN_W), lambda b, t: (t0 + b * tiles + t, 0)),
                  pl.BlockSpec((dec_seq, KV_W), lambda b, t: (s0 + b, 0)),
                  pl.BlockSpec((dec_seq, 2 * KV_W), lambda b, t: (s0 + b, 0)),
                  pl.BlockSpec((None, None, past, KV_W), lambda b, t: (b, layer, 0, 0)),
                  pl.BlockSpec((None, None, past, KV_W), lambda b, t: (b, layer, 0, 0))],
        out_specs=pl.BlockSpec((tq, ATTN_W), lambda b, t: (b * tiles + t, 0)),
        out_shape=jax.ShapeDtypeStruct((dec_batch * dec_seq, ATTN_W), BF16),
        scratch_shapes=[pltpu.VMEM((KV_W, past + dec_seq), BF16),
                        pltpu.VMEM((past, 2 * KV_W), BF16)],
        compiler_params=_params(2),
        name="attn_latent",
    )(q, k, v, cache_k, cache_v)


def _bf16_split3(x):
    c1 = x.astype(BF16).astype(F32)
    r = x - c1
    c2 = r.astype(BF16).astype(F32)
    c3 = (r - c2).astype(BF16).astype(F32)
    return c1, c2, c3


def _ssd_direction(x_ref, dt_ref, alog, h_ref, y_ref, ecol_ref, ew_ref, d, r0):
    q = SSM_CHUNK
    heads_per_group = SSM_HEADS // SSM_GROUPS
    dt = dt_ref[r0:r0 + q, :]
    a = -jnp.exp(alog)
    row = lax.broadcasted_iota(jnp.int32, (q, q), 0)
    col = lax.broadcasted_iota(jnp.int32, (q, q), 1)
    mask = (row >= col) if d == 0 else (row <= col)
    ones_tri = jnp.where(mask, 1.0, 0.0).astype(BF16)
    lane = lax.broadcasted_iota(jnp.int32, (q, LANES), 1)

    def pick_split(parts):
        return jnp.where(lane < SSM_HEADS, parts[0],
                         jnp.where(lane < 2 * SSM_HEADS, parts[1],
                                   jnp.where(lane < 3 * SSM_HEADS, parts[2], 0.0))).astype(BF16)

    da = jnp.concatenate(_bf16_split3(dt * a), axis=1).astype(BF16)
    cs = jnp.dot(ones_tri, da, preferred_element_type=F32)
    cum = (cs[:, 0:LANES] + cs[:, LANES:2 * LANES] + cs[:, 2 * LANES:3 * LANES]) * LOG2E
    cum_t = cum.T
    dt_t = dt.T
    total = cum[q - 1:q, :] if d == 0 else cum[0:1, :]
    state_w = dt * jnp.exp2(total - cum)
    cum_split = pick_split(_bf16_split3(cum))
    sw_split = pick_split(_bf16_split3(state_w))
    lo = lane < SSM_HEAD_DIM
    lo_row = lo[0:1, :]
    last = q - 1 if d == 0 else 0
    row2 = lax.broadcasted_iota(jnp.int32, (2 * q, LANES), 0)
    lane2 = lax.broadcasted_iota(jnp.int32, (2 * q, LANES), 1)
    pair_mask = jnp.where((row2 < q) == (lane2 < SSM_HEAD_DIM), 1.0, 0.0).astype(BF16)

    for g in range(SSM_GROUPS):
        b0 = D_INNER + g * SSM_STATE
        c0 = D_INNER + SSM_GROUPS * SSM_STATE + g * SSM_STATE
        bg_t = x_ref[r0:r0 + q, b0:b0 + SSM_STATE].astype(F32).T.astype(BF16)
        cg = x_ref[r0:r0 + q, c0:c0 + SSM_STATE]
        cb = jnp.dot(cg, bg_t, preferred_element_type=F32)
        h_t = h_ref[d, g]
        col_g = jnp.dot(cum_split, ecol_ref[:, g * heads_per_group * LANES:(g + 1) * heads_per_group * LANES],
                        preferred_element_type=F32)
        sw_g = jnp.dot(sw_split, ew_ref[:, g * GROUP_W:(g + 1) * GROUP_W], preferred_element_type=F32)
        y_off = jnp.dot(cg, h_t.astype(BF16), preferred_element_type=F32)
        xw_parts = []
        dec_parts = []
        for pp in range(heads_per_group // 2):
            h0 = g * heads_per_group + 2 * pp
            h1 = h0 + 1
            col0 = col_g[:, 2 * pp * LANES:(2 * pp + 1) * LANES]
            col1 = col_g[:, (2 * pp + 1) * LANES:(2 * pp + 2) * LANES]
            w0 = cb * jnp.exp2(jnp.where(mask, col0 - cum_t[h0:h0 + 1, :], -jnp.inf)) * dt_t[h0:h0 + 1, :]
            w1 = cb * jnp.exp2(jnp.where(mask, col1 - cum_t[h1:h1 + 1, :], -jnp.inf)) * dt_t[h1:h1 + 1, :]
            x0 = h0 * SSM_HEAD_DIM
            xpair = x_ref[r0:r0 + q, x0:x0 + LANES]
            w01 = jnp.concatenate([w0.astype(BF16), w1.astype(BF16)], axis=1)
            x01 = jnp.concatenate([xpair, xpair], axis=0) * pair_mask
            y_diag = jnp.dot(w01, x01, preferred_element_type=F32)
            colsel = jnp.where(lo, col0, col1)
            y_pair = y_diag + y_off[:, pp * LANES:(pp + 1) * LANES] * jnp.exp2(colsel)
            y_ref[r0:r0 + q, x0:x0 + LANES] = y_pair.astype(BF16)
            xw_parts.append((xpair.astype(F32) * sw_g[:, pp * LANES:(pp + 1) * LANES]).astype(BF16))
            dec_parts.append(jnp.exp2(jnp.where(lo_row, col0[last:last + 1, :], col1[last:last + 1, :])))
        xw = jnp.concatenate(xw_parts, axis=1)
        decay = jnp.concatenate(dec_parts, axis=1)
        h_ref[d, g] = h_t * decay + jnp.dot(bg_t, xw, preferred_element_type=F32)


def _ssd_kernel(chf_ref, chb_ref, first_ref, last_ref, islat_ref, latb_ref, ctxb_ref,
                xf_ref, xb_ref, dtf_ref, dtb_ref, alog_ref, init_ref, ecol_ref, ew_ref,
                yf_ref, yb_ref, fin_ref, h_ref):
    del chf_ref, chb_ref, latb_ref, ctxb_ref
    step = pl.program_id(0)

    @pl.when((first_ref[step] == 1) & (islat_ref[step] == 0))
    def _():
        h_ref[...] = jnp.zeros(h_ref.shape, F32)

    @pl.when((first_ref[step] == 1) & (islat_ref[step] == 1))
    def _():
        for d in range(2):
            for g in range(SSM_GROUPS):
                h_ref[d, g] = init_ref[d, g * GROUP_W:(g + 1) * GROUP_W, :].T

    for sub in range(SSD_CHUNKS_PER_STEP):
        _ssd_direction(xf_ref, dtf_ref, alog_ref[0], h_ref, yf_ref, ecol_ref, ew_ref, 0, sub * SSM_CHUNK)
        _ssd_direction(xb_ref, dtb_ref, alog_ref[1], h_ref, yb_ref, ecol_ref, ew_ref, 1,
                       (SSD_CHUNKS_PER_STEP - 1 - sub) * SSM_CHUNK)

    @pl.when((last_ref[step] == 1) & (islat_ref[step] == 0))
    def _():
        for d in range(2):
            for g in range(SSM_GROUPS):
                fin_ref[d, g * GROUP_W:(g + 1) * GROUP_W, :] = h_ref[d, g].T


def _expansion_matrices():
    r = np.arange(LANES)
    head = r % SSM_HEADS
    live = r < DT_COPIES * SSM_HEADS
    ecol = (live[:, None] & (head[:, None] == (np.arange(SSM_HEADS * LANES) // LANES)[None, :]))
    ew = (live[:, None] & (head[:, None] == (np.arange(D_INNER) // SSM_HEAD_DIM)[None, :]))
    return jnp.asarray(ecol, BF16), jnp.asarray(ew, BF16)


def _ssd_call(layer, xbc, dt, a_log, state_ssd, tabs, new_s):
    n_tok = xbc.shape[0]
    q = SSD_CHUNKS_PER_STEP * SSM_CHUNK
    n_steps = n_tok // q
    hpn = SSM_HEADS * SSM_HEAD_DIM
    dec_batch, depth = state_ssd.shape[0], state_ssd.shape[1]
    batch = int(tabs["batch"])
    init = state_ssd.reshape(dec_batch, depth, 2, hpn, SSM_STATE)
    ecol, ew = _expansion_matrices()

    in_specs = [
        pl.BlockSpec((q, CONV_CH), lambda j, cf, cb, *_: (cf[j], 0)),
        pl.BlockSpec((q, CONV_CH), lambda j, cf, cb, *_: (cb[j], 0)),
        pl.BlockSpec((q, LANES), lambda j, cf, cb, *_: (cf[j], 0)),
        pl.BlockSpec((q, LANES), lambda j, cf, cb, *_: (cb[j], 1)),
        _layer_block(a_log, layer),
        pl.BlockSpec((None, None, 2, hpn, SSM_STATE),
                     lambda j, cf, cb, fi, la, il, lb, xb: (lb[j], layer, 0, 0, 0)),
        _resident(ecol.shape, lambda j, *_: (0, 0)),
        _resident(ew.shape, lambda j, *_: (0, 0)),
    ]
    in_specs.append(pl.BlockSpec(memory_space=pl.ANY))
    args = [xbc, xbc, dt, dt, a_log, init, ecol, ew, new_s]
    n_prefetch = 7
    grid_spec = pltpu.PrefetchScalarGridSpec(
        num_scalar_prefetch=n_prefetch,
        grid=(n_steps,),
        in_specs=in_specs,
        out_specs=[
            pl.BlockSpec((q, D_INNER), lambda j, cf, cb, *_: (cf[j], 0)),
            pl.BlockSpec((q, D_INNER), lambda j, cf, cb, *_: (cb[j], 0)),
            pl.BlockSpec((None, None, 2, hpn, SSM_STATE),
                         lambda j, cf, cb, fi, la, il, lb, xb: (xb[j], layer, 0, 0, 0)),
        ],
        scratch_shapes=[pltpu.VMEM((2, SSM_GROUPS, SSM_STATE, GROUP_W), F32)],
    )

    alias_in = n_prefetch + len(args) - 1

    def body(*refs):
        _ssd_kernel(*(refs[:alias_in] + refs[alias_in + 1:]))

    return pl.pallas_call(
        body,
        grid_spec=grid_spec,
        out_shape=[jax.ShapeDtypeStruct((n_tok, D_INNER), BF16),
                   jax.ShapeDtypeStruct((n_tok, D_INNER), BF16),
                   jax.ShapeDtypeStruct(new_s.shape, F32)],
        input_output_aliases={alias_in: 2},
        compiler_params=_params(1),
        name="ssd_scan",
    )(tabs["s_chunk_f"], tabs["s_chunk_b"], tabs["s_first"], tabs["s_last"], tabs["s_islat"],
      tabs["s_latb"], tabs["s_ctxb"], *args)


COMB_SPLITS = 1
COMB_ROW_TILE = 512


def _comb_kernel(crow_ref, attn_c_ref, attn_l_ref, yf_ref, yb_ref, xs_ref, z_ref, ga_ref, gs_ref, x_ref, mod_ref,
                 dskip_ref, nw_ref, wa_ref, ws_ref, wo_ref, x1_ref, h2_ref, *, n_ctx_tiles):
    i = pl.program_id(0)
    mod = mod_ref[pl.ds(crow_ref[i], 1), :]
    gate1 = mod[:, 2 * D_MODEL:3 * D_MODEL]
    shift2 = mod[:, 3 * D_MODEL:4 * D_MODEL]
    scale2 = mod[:, 4 * D_MODEL:5 * D_MODEL]
    is_ctx = i < n_ctx_tiles
    rows = x_ref.shape[0] // COMB_SPLITS
    for r in range(COMB_SPLITS):
        sl = slice(r * rows, (r + 1) * rows)
        attn = jnp.where(is_ctx, attn_c_ref[sl, :], attn_l_ref[sl, :])
        y = yf_ref[sl, :].astype(F32) + yb_ref[sl, :].astype(F32) + dskip_ref[...] * xs_ref[sl, :].astype(F32)
        y = y * z_ref[sl, :].astype(F32)
        yn = (y * lax.rsqrt(jnp.mean(y * y, axis=-1, keepdims=True) + EPS)) * nw_ref[...]
        merged = (ga_ref[sl, :].astype(F32) * jnp.dot(attn, wa_ref[...], preferred_element_type=F32)
                  + gs_ref[sl, :].astype(F32) * jnp.dot(yn.astype(BF16), ws_ref[...], preferred_element_type=F32))
        x1 = x_ref[sl, :] + gate1 * jnp.dot(merged.astype(BF16), wo_ref[...], preferred_element_type=F32)
        x1_ref[sl, :] = x1
        h2_ref[sl, :] = _modulate(x1, shift2, scale2).astype(BF16)


def _comb_call(layer, tabs, attn_c, attn_l, yf, yb, xbc, z, ga, gs, x, mods, d_skip, ssd_norm, w_attn_o, w_ssd_o,
               w_out):
    n_tok = x.shape[0]
    tm = COMB_ROW_TILE
    nct = attn_c.shape[0] // tm
    assert attn_c.shape[0] % tm == 0 and attn_l.shape[0] % tm == 0

    def row(width):
        return pl.BlockSpec((tm, width), lambda i, *_: (i, 0))

    grid_spec = pltpu.PrefetchScalarGridSpec(
        num_scalar_prefetch=1,
        grid=(n_tok // tm,),
        in_specs=[pl.BlockSpec((tm, ATTN_W), lambda i, *_: (jnp.minimum(i, nct - 1), 0)),
                  pl.BlockSpec((tm, ATTN_W), lambda i, *_: (jnp.maximum(i - nct, 0), 0)),
                  row(D_INNER), row(D_INNER), row(D_INNER), row(D_INNER),
                  row(D_MODEL), row(D_MODEL), row(D_MODEL),
                  _layer_block(mods, layer), _layer_block(d_skip, layer), _layer_block(ssd_norm, layer),
                  _layer_block(w_attn_o, layer, resident=True),
                  _layer_block(w_ssd_o, layer, resident=True),
                  _layer_block(w_out, layer, resident=True)],
        out_specs=[row(D_MODEL), row(D_MODEL)],
    )
    return pl.pallas_call(
        functools.partial(_comb_kernel, n_ctx_tiles=nct),
        grid_spec=grid_spec,
        out_shape=[jax.ShapeDtypeStruct((n_tok, D_MODEL), F32),
                   jax.ShapeDtypeStruct((n_tok, D_MODEL), BF16)],
        compiler_params=_params(1),
        name="combine",
    )(tabs["crow_comb"], attn_c, attn_l, yf, yb, xbc, z, ga, gs, x, mods, d_skip, ssd_norm, w_attn_o, w_ssd_o,
      w_out)


FFN_COL_CHUNK = D_FF // 2


def _ffn_kernel(crow_ref, hprev_ref, hnext_ref,
                h_ref, hp_ref, hn_ref, x_ref, mod_ref, cw_ref, cb_ref, wu_ref, wd_ref,
                o_ref, sv_ref, sg_ref):
    i = pl.program_id(0)
    tm = h_ref.shape[0]
    halo = hp_ref.shape[0]
    mod = mod_ref[pl.ds(crow_ref[i], 1), :]
    gate2 = mod[:, 5 * D_MODEL:6 * D_MODEL]
    hp = (hp_ref[...].astype(F32) * hprev_ref[i].astype(F32)).astype(BF16)
    hn = (hn_ref[...].astype(F32) * hnext_ref[i].astype(F32)).astype(BF16)
    h_ext = jnp.concatenate([hp, h_ref[...], hn], axis=0)
    acc = jnp.zeros((tm, D_MODEL), F32)
    cc = FFN_COL_CHUNK
    for c in range(D_FF // cc):
        halves = []
        for s_ref, base in ((sv_ref, c * cc), (sg_ref, D_FF + c * cc)):
            s_ref[...] = jnp.dot(h_ext, wu_ref[:, base:base + cc], preferred_element_type=F32)
            u = (cb_ref[:, base:base + cc]
                 + cw_ref[0:1, base:base + cc] * s_ref[halo - 1:halo - 1 + tm, :]
                 + cw_ref[1:2, base:base + cc] * s_ref[halo:halo + tm, :]
                 + cw_ref[2:3, base:base + cc] * s_ref[halo + 1:halo + 1 + tm, :])
            halves.append(u)
        act = (_silu(halves[1]) * halves[0]).astype(BF16)
        acc = acc + jnp.dot(act, wd_ref[c * cc:(c + 1) * cc, :], preferred_element_type=F32)
    o_ref[...] = x_ref[...] + gate2 * acc


def _ffn_call(layer, tabs, h2, x1, mods, ffn_conv_w, ffn_conv_b, w_up, w_down):
    n_tok = x1.shape[0]
    tm = ROW_TILE
    halo = SUBLANES_BF16
    hb = tm // halo
    last_hb = n_tok // halo - 1

    grid_spec = pltpu.PrefetchScalarGridSpec(
        num_scalar_prefetch=3,
        grid=(n_tok // tm,),
        in_specs=[pl.BlockSpec((tm, D_MODEL), lambda i, *_: (i, 0)),
                  pl.BlockSpec((halo, D_MODEL), lambda i, *_: (jnp.maximum(i * hb - 1, 0), 0)),
                  pl.BlockSpec((halo, D_MODEL), lambda i, *_: (jnp.minimum((i + 1) * hb, last_hb), 0)),
                  pl.BlockSpec((tm, D_MODEL), lambda i, *_: (i, 0)),
                  _layer_block(mods, layer), _layer_block(ffn_conv_w, layer), _layer_block(ffn_conv_b, layer),
                  _layer_block(w_up, layer, resident=True),
                  _layer_block(w_down, layer, resident=True)],
        out_specs=pl.BlockSpec((tm, D_MODEL), lambda i, *_: (i, 0)),
        scratch_shapes=[pltpu.VMEM((tm + 2 * halo, FFN_COL_CHUNK), F32),
                        pltpu.VMEM((tm + 2 * halo, FFN_COL_CHUNK), F32)],
    )
    return pl.pallas_call(
        _ffn_kernel,
        grid_spec=grid_spec,
        out_shape=jax.ShapeDtypeStruct((n_tok, D_MODEL), F32),
        compiler_params=_params(1),
        name="conv_ffn",
    )(tabs["crow"], tabs["hprev"], tabs["hnext"], h2, h2, h2, x1, mods, ffn_conv_w, ffn_conv_b, w_up, w_down)


def _tables(batch, seq, dec_batch, dec_seq):
    tm = ROW_TILE
    crow, hprev, hnext, rblk = [], [], [], []
    for b in range(batch):
        for t in range(seq // tm):
            crow.append(0)
            hprev.append(int(t > 0))
            hnext.append(int(t < seq // tm - 1))
            rblk.append(0)
    for b in range(dec_batch):
        for t in range(dec_seq // tm):
            crow.append(1 + b)
            hprev.append(int(t > 0))
            hnext.append(int(t < dec_seq // tm - 1))
            rblk.append(1 + t)
    q = SSD_CHUNKS_PER_STEP * SSM_CHUNK
    assert seq % q == 0 and dec_seq % q == 0
    seqs = [(False, b, seq // q) for b in range(batch)] + [(True, b, dec_seq // q) for b in range(dec_batch)]
    s_chunk_f, s_chunk_b, s_first, s_last, s_islat, s_latb, s_ctxb = [], [], [], [], [], [], []
    base = 0
    for is_lat, b, n in seqs:
        for c in range(n):
            s_chunk_f.append(base + c)
            s_chunk_b.append(base + n - 1 - c)
            s_first.append(int(c == 0))
            s_last.append(int(c == n - 1))
            s_islat.append(int(is_lat))
            s_latb.append(b if is_lat else 0)
            s_ctxb.append(batch - 1 if is_lat else b)
        base += n
    per = COMB_ROW_TILE // tm
    assert len(crow) % per == 0 and all(len(set(crow[i:i + per])) == 1 for i in range(0, len(crow), per))
    crow_comb = crow[::per]
    tabs = dict(crow=crow, crow_comb=crow_comb, hprev=hprev, hnext=hnext, rblk=rblk, s_chunk_f=s_chunk_f,
                s_chunk_b=s_chunk_b,
                s_first=s_first, s_last=s_last, s_islat=s_islat, s_latb=s_latb, s_ctxb=s_ctxb)
    tabs = {k: jnp.asarray(np.asarray(v, np.int32)) for k, v in tabs.items()}
    tabs["batch"] = batch
    return tabs


def _rope_tables(dec_seq):
    rows = dec_seq // GRID_W
    row = jnp.repeat(jnp.arange(rows), GRID_W).astype(F32)
    col = jnp.tile(jnp.arange(GRID_W), rows).astype(F32)
    half = HEAD_DIM // 2
    inv_freq = ROPE_THETA ** (-jnp.arange(0, half, 2, dtype=F32) / half)
    ang_r = row[:, None] * inv_freq
    ang_c = col[:, None] * inv_freq
    ang = jnp.concatenate([ang_r, ang_r, ang_c, ang_c], axis=-1)
    sign = jnp.where((jnp.arange(HEAD_DIM) & (HEAD_DIM // 4)) == 0, -1.0, 1.0).astype(F32)
    cos = jnp.concatenate([jnp.ones((ROW_TILE, HEAD_DIM), F32), jnp.cos(ang)], axis=0)
    sin = jnp.concatenate([jnp.zeros((ROW_TILE, HEAD_DIM), F32), jnp.sin(ang) * sign], axis=0)
    return cos, sin


def _dir_lanes(p):
    tail = jnp.zeros(p.shape[:-1] + (LANES - DT_COPIES * SSM_HEADS,), p.dtype)
    return jnp.concatenate([p] * DT_COPIES + [tail], axis=-1)


def _cast_kernel(w_ref, o_ref):
    o_ref[...] = w_ref[...].astype(BF16)


def _cast_call(w, n_cols, tn):
    depth, k, _ = w.shape
    assert n_cols % tn == 0 and tn % LANES == 0
    return pl.pallas_call(
        _cast_kernel,
        grid=(depth, n_cols // tn),
        in_specs=[pl.BlockSpec((1, k, tn), lambda l, j: (l, 0, j))],
        out_specs=pl.BlockSpec((1, k, tn), lambda l, j: (l, 0, j)),
        out_shape=jax.ShapeDtypeStruct((depth, k, n_cols), BF16),
        compiler_params=_params(2),
        name="cast_bf16",
    )(w)


def _cast_t_kernel(w_ref, o_ref):
    o_ref[0] = w_ref[0].T.astype(BF16)


def _cast_t_call(w_t, n_cols, tn):
    depth, _, k = w_t.shape
    assert n_cols % tn == 0 and tn % LANES == 0
    return pl.pallas_call(
        _cast_t_kernel,
        grid=(depth, n_cols // tn),
        in_specs=[pl.BlockSpec((1, tn, k), lambda l, j: (l, j, 0))],
        out_specs=pl.BlockSpec((1, k, tn), lambda l, j: (l, 0, j)),
        out_shape=jax.ShapeDtypeStruct((depth, k, n_cols), BF16),
        compiler_params=_params(2),
        name="cast_t_bf16",
    )(w_t)


def _split_w_in(w_in):
    depth = w_in.shape[0]
    w_t = jnp.swapaxes(w_in, 1, 2)
    w_dt = jnp.swapaxes(w_t[:, DT0:DT0 + 2 * SSM_HEADS, :], 1, 2).reshape(depth, D_MODEL, 2, SSM_HEADS)
    w_dt = _dir_lanes(w_dt).reshape(depth, D_MODEL, 2 * LANES)
    w_gates = _cast_t_call(w_t[:, DT0 + 2 * SSM_HEADS:, :], 2 * D_MODEL, D_MODEL)
    return _cast_t_call(w_t, DT0, DT0 // 4), w_dt.astype(BF16), w_gates


def kernel(x_prompt, x_sample, c, cache_k, cache_v, state_ssd, c_ctx, w_mod, b_mod, w_in, q_norm, k_norm,
           conv_w, conv_b, dt_bias, a_log, d_skip, ssd_norm, w_attn_o, w_ssd_o, w_out, w_up,
           ffn_conv_w, ffn_conv_b, w_down):
    batch, seq, _ = x_prompt.shape
    dec_batch, dec_seq, _ = x_sample.shape
    depth = w_in.shape[0]
    past = cache_k.shape[2]
    n_ctx = batch * seq
    assert seq % ROW_TILE == 0 and dec_seq % ROW_TILE == 0 and n_ctx % dec_seq == 0
    assert 1 + dec_batch <= SUBLANES_F32

    tabs = _tables(batch, seq, dec_batch, dec_seq)
    rope_cos, rope_sin = _rope_tables(dec_seq)
    cond = jnp.concatenate([c_ctx[None, :], c, jnp.zeros((SUBLANES_F32 - 1 - dec_batch, D_MODEL), F32)], axis=0)
    mods = _mod_call(cond, w_mod, b_mod)

    w_main, w_dt, w_gates = _split_w_in(w_in)
    wa = _cast_call(w_attn_o, D_MODEL, D_MODEL)
    ws = _cast_call(w_ssd_o, D_MODEL, D_MODEL // 2)
    wo = _cast_call(w_out, D_MODEL, D_MODEL)
    wu = _cast_call(w_up, 2 * D_FF, D_FF // 2)
    wd = _cast_call(w_down, D_MODEL, D_MODEL // 2)
    qn = q_norm.reshape(depth, 1, HEAD_DIM)
    kn = k_norm.reshape(depth, 1, HEAD_DIM)
    cvb = conv_b.reshape(depth, 1, CONV_CH)
    fcb = ffn_conv_b.reshape(depth, 1, 2 * D_FF)
    dtb = _dir_lanes(dt_bias).reshape(depth, 1, 2 * LANES)
    alog = _dir_lanes(a_log).reshape(depth, 2, 1, LANES)
    dsk = jnp.repeat(d_skip, SSM_HEAD_DIM, axis=-1).reshape(depth, 1, D_INNER)
    nrm = ssd_norm.reshape(depth, 1, D_INNER)
    ck = cache_k.reshape(dec_batch, depth, past, KV_W)
    cv = cache_v.reshape(dec_batch, depth, past, KV_W)

    x = jnp.concatenate([x_prompt.reshape(n_ctx, D_MODEL), x_sample.reshape(dec_batch * dec_seq, D_MODEL)], axis=0)
    new_k = jnp.zeros((batch, depth, seq, KV_W), F32)
    new_v = jnp.zeros((batch, depth, seq, KV_W), F32)
    new_s = jnp.zeros((batch, depth, 2, SSM_HEADS * SSM_HEAD_DIM, SSM_STATE), F32)
    for l in range(depth):
        q, k, v, kf, vf, z, xbc, dt, ga, gs = _in_call(
            l, x, mods, tabs, rope_cos, rope_sin, qn, kn, conv_w, cvb, dtb, w_main, w_dt, w_gates)
        attn_c, new_k, new_v = _attn_ctx_call(l, q, k, v, kf, vf, new_k, new_v, batch, seq)
        attn_l = _attn_lat_call(q, k, v, ck, cv, n_ctx, dec_batch, dec_seq, l)
        yf, yb, new_s = _ssd_call(l, xbc, dt, alog, state_ssd, tabs, new_s)
        x1, h2 = _comb_call(l, tabs, attn_c, attn_l, yf, yb, xbc, z, ga, gs, x, mods, dsk, nrm, wa, ws, wo)
        x = _ffn_call(l, tabs, h2, x1, mods, ffn_conv_w, fcb, wu, wd)
    y_prompt = x[:n_ctx].reshape(batch, seq, D_MODEL)
    y_sample = x[n_ctx:].reshape(dec_batch, dec_seq, D_MODEL)
    new_state = new_s.reshape(batch, depth, 2, SSM_HEADS, SSM_HEAD_DIM, SSM_STATE)
    return (y_prompt, y_sample, new_k.reshape(batch, depth, seq, H_KV, HEAD_DIM),
            new_v.reshape(batch, depth, seq, H_KV, HEAD_DIM), new_state)
```

```python
import functools
import math

import numpy as np
import jax
import jax.numpy as jnp
from jax import lax
from jax.experimental import pallas as pl
from jax.experimental.pallas import tpu as pltpu

F32 = jnp.float32
BF16 = jnp.bfloat16

D_MODEL = 1024
GRID_W = 64
EPS = 1e-6
H_Q = 8
H_KV = 2
HEAD_DIM = 128
ATTN_W = H_Q * HEAD_DIM
KV_W = H_KV * HEAD_DIM
ROPE_THETA = 10000.0
D_INNER = 2 * D_MODEL
SSM_HEAD_DIM = 64
SSM_HEADS = D_INNER // SSM_HEAD_DIM
SSM_GROUPS = 4
SSM_STATE = 128
SSM_CHUNK = 128
CONV_CH = D_INNER + 2 * SSM_GROUPS * SSM_STATE
D_FF = 2816
GROUP_W = (SSM_HEADS // SSM_GROUPS) * SSM_HEAD_DIM
LOG2E = 1.4426950408889634

LANES = 128
SUBLANES_F32 = 8
SUBLANES_BF16 = 16
VMEM_LIMIT_BYTES = 56 * 1024 * 1024

Q0 = 0
K0 = Q0 + ATTN_W
V0 = K0 + KV_W
Z0 = V0 + KV_W
X0 = Z0 + D_INNER
DT0 = X0 + CONV_CH
DT_COPIES = 3

ROW_TILE = 256
SSD_CHUNKS_PER_STEP = 2
ATTN_Q_TILE = 512


def _sigmoid(x):
    return 0.5 * jnp.tanh(0.5 * x) + 0.5


def _silu(x):
    h = 0.5 * x
    return h * jnp.tanh(h) + h


def _params(n_axes):
    return pltpu.CompilerParams(dimension_semantics=("arbitrary",) * n_axes,
                                vmem_limit_bytes=VMEM_LIMIT_BYTES)


def _resident(shape, index_map):
    return pl.BlockSpec(shape, index_map, pipeline_mode=pl.Buffered(1))


def _layer_block(arr, layer, resident=False):
    shape = (None,) + tuple(arr.shape[1:])
    nz = len(arr.shape) - 1

    def index_map(*_):
        return (layer,) + (0,) * nz

    return _resident(shape, index_map) if resident else pl.BlockSpec(shape, index_map)


def _mod_kernel(cond_ref, w_ref, b_ref, o_ref):
    a = _silu(cond_ref[...]).astype(BF16)
    o_ref[0] = jnp.dot(a, w_ref[0].astype(BF16), preferred_element_type=F32) + b_ref[0]


def _mod_call(cond, w_mod, b_mod):
    depth, _, n = w_mod.shape
    tn = 1536
    return pl.pallas_call(
        _mod_kernel,
        grid=(depth, n // tn),
        in_specs=[pl.BlockSpec(cond.shape, lambda l, j: (0, 0)),
                  pl.BlockSpec((1, D_MODEL, tn), lambda l, j: (l, 0, j)),
                  pl.BlockSpec((1, 1, tn), lambda l, j: (l, 0, j))],
        out_specs=pl.BlockSpec((1, cond.shape[0], tn), lambda l, j: (l, 0, j)),
        out_shape=jax.ShapeDtypeStruct((depth, cond.shape[0], n), F32),
        compiler_params=_params(2),
        name="mod_vectors",
    )(cond, w_mod, b_mod.reshape(depth, 1, n))


def _modulate(x, shift, scale):
    ms = jnp.mean(x * x, axis=-1, keepdims=True)
    return (x * lax.rsqrt(ms + EPS)) * (1.0 + scale) + shift


def _in_kernel(crow_ref, hprev_ref, hnext_ref, rblk_ref,
               x_ref, xp_ref, xn_ref, mod_ref, cos_ref, sin_ref, qn_ref, kn_ref, cw_ref, cb_ref,
               dtb_ref, w_ref, wdt_ref, wg_ref,
               q_ref, k_ref, v_ref, kf_ref, vf_ref, z_ref, xbc_ref, dt_ref, ga_ref, gs_ref,
               s_ref):
    del rblk_ref
    i = pl.program_id(0)
    tm = x_ref.shape[0]
    halo = xp_ref.shape[0]
    mod = mod_ref[pl.ds(crow_ref[i], 1), :]
    shift = mod[:, 0:D_MODEL]
    scale = mod[:, D_MODEL:2 * D_MODEL]
    hm = _modulate(x_ref[...], shift, scale)
    hp = _modulate(xp_ref[...], shift, scale) * hprev_ref[i].astype(F32)
    hn = _modulate(xn_ref[...], shift, scale) * hnext_ref[i].astype(F32)
    h = hm.astype(BF16)
    h_ext = jnp.concatenate([hp, hm, hn], axis=0).astype(BF16)

    qkv = jnp.dot(h, w_ref[:, Q0:Z0], preferred_element_type=F32)
    cos = cos_ref[...]
    sin = sin_ref[...]
    lane = lax.broadcasted_iota(jnp.int32, (tm, HEAD_DIM), 1)
    first_quarter = (lane & (HEAD_DIM // 4)) == 0

    def norm_rope(u, w):
        un = u * lax.rsqrt(jnp.mean(u * u, axis=-1, keepdims=True) + EPS) * w
        rot = jnp.where(first_quarter, pltpu.roll(un, HEAD_DIM - HEAD_DIM // 4, 1),
                        pltpu.roll(un, HEAD_DIM // 4, 1))
        return un * cos + rot * sin

    q_scale = HEAD_DIM ** -0.5 * LOG2E
    for j in range(H_Q):
        qh = norm_rope(qkv[:, j * HEAD_DIM:(j + 1) * HEAD_DIM], qn_ref[...])
        q_ref[:, j * HEAD_DIM:(j + 1) * HEAD_DIM] = (qh * q_scale).astype(BF16)
    ones = jnp.ones((tm, HEAD_DIM), BF16)
    for j in range(H_KV):
        kh = norm_rope(qkv[:, K0 + j * HEAD_DIM:K0 + (j + 1) * HEAD_DIM], kn_ref[...])
        k_ref[:, j * HEAD_DIM:(j + 1) * HEAD_DIM] = kh.astype(BF16)
        kf_ref[:, j * HEAD_DIM:(j + 1) * HEAD_DIM] = kh
        v_ref[:, 2 * j * HEAD_DIM:(2 * j + 1) * HEAD_DIM] = qkv[:, V0 + j * HEAD_DIM:V0 + (j + 1) * HEAD_DIM].astype(BF16)
        v_ref[:, (2 * j + 1) * HEAD_DIM:(2 * j + 2) * HEAD_DIM] = ones
    vf_ref[...] = qkv[:, V0:Z0]

    z_ref[...] = _silu(jnp.dot(h, w_ref[:, Z0:X0], preferred_element_type=F32)).astype(BF16)

    s_ref[...] = jnp.dot(h_ext, w_ref[:, X0:DT0], preferred_element_type=F32)
    xc = (cb_ref[...] + cw_ref[0:1, :] * s_ref[halo - 1:halo - 1 + tm, :]
          + cw_ref[1:2, :] * s_ref[halo:halo + tm, :]
          + cw_ref[2:3, :] * s_ref[halo + 1:halo + 1 + tm, :])
    xbc_ref[...] = _silu(xc).astype(BF16)

    dt_raw = jnp.dot(h, wdt_ref[...], preferred_element_type=F32) + dtb_ref[...]
    dt_ref[...] = jnp.maximum(dt_raw, 0.0) + jnp.log1p(jnp.exp(-jnp.abs(dt_raw)))

    gates = _sigmoid(jnp.dot(h, wg_ref[...], preferred_element_type=F32)).astype(BF16)
    ga_ref[...] = gates[:, 0:D_MODEL]
    gs_ref[...] = gates[:, D_MODEL:2 * D_MODEL]


def _in_call(layer, x, mods, tabs, rope_cos, rope_sin, q_norm, k_norm, conv_w, conv_b, dt_bias, w_main, w_dt,
             w_gates):
    n_tok = x.shape[0]
    tm = ROW_TILE
    halo = SUBLANES_F32
    n_tiles = n_tok // tm
    hb = tm // halo
    last_hb = n_tok // halo - 1

    def row(width):
        return pl.BlockSpec((tm, width), lambda i, *_: (i, 0))

    grid_spec = pltpu.PrefetchScalarGridSpec(
        num_scalar_prefetch=4,
        grid=(n_tiles,),
        in_specs=[
            row(D_MODEL),
            pl.BlockSpec((halo, D_MODEL), lambda i, *_: (jnp.maximum(i * hb - 1, 0), 0)),
            pl.BlockSpec((halo, D_MODEL), lambda i, *_: (jnp.minimum((i + 1) * hb, last_hb), 0)),
            _layer_block(mods, layer),
            pl.BlockSpec((tm, HEAD_DIM), lambda i, c, p, n, r: (r[i], 0)),
            pl.BlockSpec((tm, HEAD_DIM), lambda i, c, p, n, r: (r[i], 0)),
            _layer_block(q_norm, layer), _layer_block(k_norm, layer),
            _layer_block(conv_w, layer), _layer_block(conv_b, layer), _layer_block(dt_bias, layer),
            _layer_block(w_main, layer, resident=True),
            _layer_block(w_dt, layer, resident=True),
            _layer_block(w_gates, layer, resident=True),
        ],
        out_specs=[row(ATTN_W), row(KV_W), row(2 * KV_W), row(KV_W), row(KV_W), row(D_INNER),
                   row(CONV_CH), row(2 * LANES), row(D_MODEL), row(D_MODEL)],
        scratch_shapes=[pltpu.VMEM((tm + 2 * halo, CONV_CH), F32)],
    )
    sds = jax.ShapeDtypeStruct
    return pl.pallas_call(
        _in_kernel,
        grid_spec=grid_spec,
        out_shape=[sds((n_tok, ATTN_W), BF16), sds((n_tok, KV_W), BF16), sds((n_tok, 2 * KV_W), BF16),
                   sds((n_tok, KV_W), F32), sds((n_tok, KV_W), F32), sds((n_tok, D_INNER), BF16),
                   sds((n_tok, CONV_CH), BF16), sds((n_tok, 2 * LANES), F32),
                   sds((n_tok, D_MODEL), BF16), sds((n_tok, D_MODEL), BF16)],
        compiler_params=_params(1),
        name="in_proj",
    )(tabs["crow"], tabs["hprev"], tabs["hnext"], tabs["rblk"],
      x, x, x, mods, rope_cos, rope_sin, q_norm, k_norm, conv_w, conv_b, dt_bias, w_main, w_dt, w_gates)


def _attn_kernel(*refs, has_cache):
    if has_cache:
        q_ref, k_ref, v_ref, kc_ref, vc_ref, o_ref, kt_ref, vcx_ref = refs
        past = kc_ref.shape[0]
    else:
        q_ref, k_ref, v_ref, kf_ref, vf_ref, _, _, o_ref, nk_ref, nv_ref, kt_ref = refs
        past = 0
        for g in range(H_KV):
            nk_ref[:, g, :] = kf_ref[:, g * HEAD_DIM:(g + 1) * HEAD_DIM]
            nv_ref[:, g, :] = vf_ref[:, g * HEAD_DIM:(g + 1) * HEAD_DIM]
    rep = H_Q // H_KV

    @pl.when(pl.program_id(1) == 0)
    def _():
        if has_cache:
            kt_ref[:, 0:past] = kc_ref[...].T.astype(BF16)
            for g in range(H_KV):
                vcx_ref[:, 2 * g * HEAD_DIM:(2 * g + 1) * HEAD_DIM] = vc_ref[:, g * HEAD_DIM:(g + 1) * HEAD_DIM].astype(BF16)
                vcx_ref[:, (2 * g + 1) * HEAD_DIM:(2 * g + 2) * HEAD_DIM] = jnp.ones((past, HEAD_DIM), BF16)
        kt_ref[:, past:] = k_ref[...].astype(F32).T.astype(BF16)

    for j in range(H_Q):
        g = j // rep
        q = q_ref[:, j * HEAD_DIM:(j + 1) * HEAD_DIM]
        s = jnp.dot(q, kt_ref[g * HEAD_DIM:(g + 1) * HEAD_DIM, :], preferred_element_type=F32)
        p = jnp.exp2(s - jnp.max(s, axis=-1, keepdims=True)).astype(BF16)
        o = jnp.dot(p[:, past:], v_ref[:, 2 * g * HEAD_DIM:(2 * g + 2) * HEAD_DIM], preferred_element_type=F32)
        if has_cache:
            o = o + jnp.dot(p[:, 0:past], vcx_ref[:, 2 * g * HEAD_DIM:(2 * g + 2) * HEAD_DIM],
                            preferred_element_type=F32)
        o_ref[:, j * HEAD_DIM:(j + 1) * HEAD_DIM] = (o[:, 0:HEAD_DIM] / o[:, HEAD_DIM:2 * HEAD_DIM]).astype(BF16)


def _attn_ctx_call(layer, q, k, v, kf, vf, new_k, new_v, batch, seq):
    n_tok = batch * seq
    stacked = pl.BlockSpec((None, None, seq, H_KV, HEAD_DIM), lambda b, t: (b, layer, 0, 0, 0))
    return pl.pallas_call(
        functools.partial(_attn_kernel, has_cache=False),
        grid=(batch, 1),
        in_specs=[pl.BlockSpec((seq, ATTN_W), lambda b, t: (b, 0)),
                  pl.BlockSpec((seq, KV_W), lambda b, t: (b, 0)),
                  pl.BlockSpec((seq, 2 * KV_W), lambda b, t: (b, 0)),
                  pl.BlockSpec((seq, KV_W), lambda b, t: (b, 0)),
                  pl.BlockSpec((seq, KV_W), lambda b, t: (b, 0)),
                  pl.BlockSpec(memory_space=pl.ANY),
                  pl.BlockSpec(memory_space=pl.ANY)],
        out_specs=[pl.BlockSpec((seq, ATTN_W), lambda b, t: (b, 0)), stacked, stacked],
        out_shape=[jax.ShapeDtypeStruct((n_tok, ATTN_W), BF16),
                   jax.ShapeDtypeStruct(new_k.shape, F32), jax.ShapeDtypeStruct(new_v.shape, F32)],
        scratch_shapes=[pltpu.VMEM((KV_W, seq), BF16)],
        input_output_aliases={5: 1, 6: 2},
        compiler_params=_params(2),
        name="attn_context",
    )(q, k, v, kf, vf, new_k, new_v)


def _attn_lat_call(q, k, v, cache_k, cache_v, n_ctx_tok, dec_batch, dec_seq, layer):
    tq = ATTN_Q_TILE
    t0 = n_ctx_tok // tq
    s0 = n_ctx_tok // dec_seq
    tiles = dec_seq // tq
    past = cache_k.shape[2]
    return pl.pallas_call(
        functools.partial(_attn_kernel, has_cache=True),
        grid=(dec_batch, tiles),
        in_specs=[pl.BlockSpec((tq, ATTN_W), lambda b, t: (t0 + b * tiles + t, 0)),
                  pl.BlockSpec((dec_seq, KV_W), lambda b, t: (s0 + b, 0)),
                  pl.BlockSpec((dec_seq, 2 * KV_W), lambda b, t: (s0 + b, 0)),
                  pl.BlockSpec((None, None, past, KV_W), lambda b, t: (b, layer, 0, 0)),
                  pl.BlockSpec((None, None, past, KV_W), lambda b, t: (b, layer, 0, 0))],
        out_specs=pl.BlockSpec((tq, ATTN_W), lambda b, t: (b * tiles + t, 0)),
        out_shape=jax.ShapeDtypeStruct((dec_batch * dec_seq, ATTN_W), BF16),
        scratch_shapes=[pltpu.VMEM((KV_W, past + dec_seq), BF16),
                        pltpu.VMEM((past, 2 * KV_W), BF16)],
        compiler_params=_params(2),
        name="attn_latent",
    )(q, k, v, cache_k, cache_v)


def _bf16_split3(x):
    c1 = x.astype(BF16).astype(F32)
    r = x - c1
    c2 = r.astype(BF16).astype(F32)
    c3 = (r - c2).astype(BF16).astype(F32)
    return c1, c2, c3


def _ssd_direction(x_ref, dt_ref, alog, h_ref, y_ref, ecol_ref, ew_ref, d, r0):
    q = SSM_CHUNK
    heads_per_group = SSM_HEADS // SSM_GROUPS
    dt = dt_ref[r0:r0 + q, :]
    a = -jnp.exp(alog)
    row = lax.broadcasted_iota(jnp.int32, (q, q), 0)
    col = lax.broadcasted_iota(jnp.int32, (q, q), 1)
    mask = (row >= col) if d == 0 else (row <= col)
    ones_tri = jnp.where(mask, 1.0, 0.0).astype(BF16)
    lane = lax.broadcasted_iota(jnp.int32, (q, LANES), 1)

    def pick_split(parts):
        return jnp.where(lane < SSM_HEADS, parts[0],
                         jnp.where(lane < 2 * SSM_HEADS, parts[1],
                                   jnp.where(lane < 3 * SSM_HEADS, parts[2], 0.0))).astype(BF16)

    da = jnp.concatenate(_bf16_split3(dt * a), axis=1).astype(BF16)
    cs = jnp.dot(ones_tri, da, preferred_element_type=F32)
    cum = (cs[:, 0:LANES] + cs[:, LANES:2 * LANES] + cs[:, 2 * LANES:3 * LANES]) * LOG2E
    cum_t = cum.T
    dt_t = dt.T
    total = cum[q - 1:q, :] if d == 0 else cum[0:1, :]
    state_w = dt * jnp.exp2(total - cum)
    cum_split = pick_split(_bf16_split3(cum))
    sw_split = pick_split(_bf16_split3(state_w))
    lo = lane < SSM_HEAD_DIM
    lo_row = lo[0:1, :]
    last = q - 1 if d == 0 else 0
    row2 = lax.broadcasted_iota(jnp.int32, (2 * q, LANES), 0)
    lane2 = lax.broadcasted_iota(jnp.int32, (2 * q, LANES), 1)
    pair_mask = jnp.where((row2 < q) == (lane2 < SSM_HEAD_DIM), 1.0, 0.0).astype(BF16)

    for g in range(SSM_GROUPS):
        b0 = D_INNER + g * SSM_STATE
        c0 = D_INNER + SSM_GROUPS * SSM_STATE + g * SSM_STATE
        bg_t = x_ref[r0:r0 + q, b0:b0 + SSM_STATE].astype(F32).T.astype(BF16)
        cg = x_ref[r0:r0 + q, c0:c0 + SSM_STATE]
        cb = jnp.dot(cg, bg_t, preferred_element_type=F32)
        h_t = h_ref[d, g]
        col_g = jnp.dot(cum_split, ecol_ref[:, g * heads_per_group * LANES:(g + 1) * heads_per_group * LANES],
                        preferred_element_type=F32)
        sw_g = jnp.dot(sw_split, ew_ref[:, g * GROUP_W:(g + 1) * GROUP_W], preferred_element_type=F32)
        y_off = jnp.dot(cg, h_t.astype(BF16), preferred_element_type=F32)
        xw_parts = []
        dec_parts = []
        for pp in range(heads_per_group // 2):
            h0 = g * heads_per_group + 2 * pp
            h1 = h0 + 1
            col0 = col_g[:, 2 * pp * LANES:(2 * pp + 1) * LANES]
            col1 = col_g[:, (2 * pp + 1) * LANES:(2 * pp + 2) * LANES]
            w0 = cb * jnp.exp2(jnp.where(mask, col0 - cum_t[h0:h0 + 1, :], -jnp.inf)) * dt_t[h0:h0 + 1, :]
            w1 = cb * jnp.exp2(jnp.where(mask, col1 - cum_t[h1:h1 + 1, :], -jnp.inf)) * dt_t[h1:h1 + 1, :]
            x0 = h0 * SSM_HEAD_DIM
            xpair = x_ref[r0:r0 + q, x0:x0 + LANES]
            w01 = jnp.concatenate([w0.astype(BF16), w1.astype(BF16)], axis=1)
            x01 = jnp.concatenate([xpair, xpair], axis=0) * pair_mask
            y_diag = jnp.dot(w01, x01, preferred_element_type=F32)
            colsel = jnp.where(lo, col0, col1)
            y_pair = y_diag + y_off[:, pp * LANES:(pp + 1) * LANES] * jnp.exp2(colsel)
            y_ref[r0:r0 + q, x0:x0 + LANES] = y_pair.astype(BF16)
            xw_parts.append((xpair.astype(F32) * sw_g[:, pp * LANES:(pp + 1) * LANES]).astype(BF16))
            dec_parts.append(jnp.exp2(jnp.where(lo_row, col0[last:last + 1, :], col1[last:last + 1, :])))
        xw = jnp.concatenate(xw_parts, axis=1)
        decay = jnp.concatenate(dec_parts, axis=1)
        h_ref[d, g] = h_t * decay + jnp.dot(bg_t, xw, preferred_element_type=F32)


def _ssd_kernel(chf_ref, chb_ref, first_ref, last_ref, islat_ref, latb_ref, ctxb_ref,
                xf_ref, xb_ref, dtf_ref, dtb_ref, alog_ref, init_ref, ecol_ref, ew_ref,
                yf_ref, yb_ref, fin_ref, h_ref):
    del chf_ref, chb_ref, latb_ref, ctxb_ref
    step = pl.program_id(0)

    @pl.when((first_ref[step] == 1) & (islat_ref[step] == 0))
    def _():
        h_ref[...] = jnp.zeros(h_ref.shape, F32)

    @pl.when((first_ref[step] == 1) & (islat_ref[step] == 1))
    def _():
        for d in range(2):
            for g in range(SSM_GROUPS):
                h_ref[d, g] = init_ref[d, g * GROUP_W:(g + 1) * GROUP_W, :].T

    for sub in range(SSD_CHUNKS_PER_STEP):
        _ssd_direction(xf_ref, dtf_ref, alog_ref[0], h_ref, yf_ref, ecol_ref, ew_ref, 0, sub * SSM_CHUNK)
        _ssd_direction(xb_ref, dtb_ref, alog_ref[1], h_ref, yb_ref, ecol_ref, ew_ref, 1,
                       (SSD_CHUNKS_PER_STEP - 1 - sub) * SSM_CHUNK)

    @pl.when((last_ref[step] == 1) & (islat_ref[step] == 0))
    def _():
        for d in range(2):
            for g in range(SSM_GROUPS):
                fin_ref[d, g * GROUP_W:(g + 1) * GROUP_W, :] = h_ref[d, g].T


def _expansion_matrices():
    r = np.arange(LANES)
    head = r % SSM_HEADS
    live = r < DT_COPIES * SSM_HEADS
    ecol = (live[:, None] & (head[:, None] == (np.arange(SSM_HEADS * LANES) // LANES)[None, :]))
    ew = (live[:, None] & (head[:, None] == (np.arange(D_INNER) // SSM_HEAD_DIM)[None, :]))
    return jnp.asarray(ecol, BF16), jnp.asarray(ew, BF16)


def _ssd_call(layer, xbc, dt, a_log, state_ssd, tabs, new_s):
    n_tok = xbc.shape[0]
    q = SSD_CHUNKS_PER_STEP * SSM_CHUNK
    n_steps = n_tok // q
    hpn = SSM_HEADS * SSM_HEAD_DIM
    dec_batch, depth = state_ssd.shape[0], state_ssd.shape[1]
    init = state_ssd.reshape(dec_batch, depth, 2, hpn, SSM_STATE)
    ecol, ew = _expansion_matrices()

    in_specs = [
        pl.BlockSpec((q, CONV_CH), lambda j, cf, cb, *_: (cf[j], 0)),
        pl.BlockSpec((q, CONV_CH), lambda j, cf, cb, *_: (cb[j], 0)),
        pl.BlockSpec((q, LANES), lambda j, cf, cb, *_: (cf[j], 0)),
        pl.BlockSpec((q, LANES), lambda j, cf, cb, *_: (cb[j], 1)),
        _layer_block(a_log, layer),
        pl.BlockSpec((None, None, 2, hpn, SSM_STATE),
                     lambda j, cf, cb, fi, la, il, lb, xb: (lb[j], layer, 0, 0, 0)),
        _resident(ecol.shape, lambda j, *_: (0, 0)),
        _resident(ew.shape, lambda j, *_: (0, 0)),
    ]
    in_specs.append(pl.BlockSpec(memory_space=pl.ANY))
    args = [xbc, xbc, dt, dt, a_log, init, ecol, ew, new_s]
    n_prefetch = 7
    grid_spec = pltpu.PrefetchScalarGridSpec(
        num_scalar_prefetch=n_prefetch,
        grid=(n_steps,),
        in_specs=in_specs,
        out_specs=[
            pl.BlockSpec((q, D_INNER), lambda j, cf, cb, *_: (cf[j], 0)),
            pl.BlockSpec((q, D_INNER), lambda j, cf, cb, *_: (cb[j], 0)),
            pl.BlockSpec((None, None, 2, hpn, SSM_STATE),
                         lambda j, cf, cb, fi, la, il, lb, xb: (xb[j], layer, 0, 0, 0)),
        ],
        scratch_shapes=[pltpu.VMEM((2, SSM_GROUPS, SSM_STATE, GROUP_W), F32)],
    )

    alias_in = n_prefetch + len(args) - 1

    def body(*refs):
        _ssd_kernel(*(refs[:alias_in] + refs[alias_in + 1:]))

    return pl.pallas_call(
        body,
        grid_spec=grid_spec,
        out_shape=[jax.ShapeDtypeStruct((n_tok, D_INNER), BF16),
                   jax.ShapeDtypeStruct((n_tok, D_INNER), BF16),
                   jax.ShapeDtypeStruct(new_s.shape, F32)],
        input_output_aliases={alias_in: 2},
        compiler_params=_params(1),
        name="ssd_scan",
    )(tabs["s_chunk_f"], tabs["s_chunk_b"], tabs["s_first"], tabs["s_last"], tabs["s_islat"],
      tabs["s_latb"], tabs["s_ctxb"], *args)


COMB_ROW_TILE = 512


def _comb_kernel(crow_ref, attn_c_ref, attn_l_ref, yf_ref, yb_ref, xs_ref, z_ref, ga_ref, gs_ref, x_ref, mod_ref,
                 dskip_ref, nw_ref, wa_ref, ws_ref, wo_ref, x1_ref, h2_ref, *, n_ctx_tiles):
    i = pl.program_id(0)
    mod = mod_ref[pl.ds(crow_ref[i], 1), :]
    gate1 = mod[:, 2 * D_MODEL:3 * D_MODEL]
    shift2 = mod[:, 3 * D_MODEL:4 * D_MODEL]
    scale2 = mod[:, 4 * D_MODEL:5 * D_MODEL]
    attn = jnp.where(i < n_ctx_tiles, attn_c_ref[...], attn_l_ref[...])
    y = yf_ref[...].astype(F32) + yb_ref[...].astype(F32) + dskip_ref[...] * xs_ref[...].astype(F32)
    y = y * z_ref[...].astype(F32)
    yn = (y * lax.rsqrt(jnp.mean(y * y, axis=-1, keepdims=True) + EPS)) * nw_ref[...]
    merged = (ga_ref[...].astype(F32) * jnp.dot(attn, wa_ref[...], preferred_element_type=F32)
              + gs_ref[...].astype(F32) * jnp.dot(yn.astype(BF16), ws_ref[...], preferred_element_type=F32))
    x1 = x_ref[...] + gate1 * jnp.dot(merged.astype(BF16), wo_ref[...], preferred_element_type=F32)
    x1_ref[...] = x1
    h2_ref[...] = _modulate(x1, shift2, scale2).astype(BF16)


def _comb_call(layer, tabs, attn_c, attn_l, yf, yb, xbc, z, ga, gs, x, mods, d_skip, ssd_norm, w_attn_o, w_ssd_o,
               w_out):
    n_tok = x.shape[0]
    tm = COMB_ROW_TILE
    nct = attn_c.shape[0] // tm
    assert attn_c.shape[0] % tm == 0 and attn_l.shape[0] % tm == 0

    def row(width):
        return pl.BlockSpec((tm, width), lambda i, *_: (i, 0))

    grid_spec = pltpu.PrefetchScalarGridSpec(
        num_scalar_prefetch=1,
        grid=(n_tok // tm,),
        in_specs=[pl.BlockSpec((tm, ATTN_W), lambda i, *_: (jnp.minimum(i, nct - 1), 0)),
                  pl.BlockSpec((tm, ATTN_W), lambda i, *_: (jnp.maximum(i - nct, 0), 0)),
                  row(D_INNER), row(D_INNER), row(D_INNER), row(D_INNER),
                  row(D_MODEL), row(D_MODEL), row(D_MODEL),
                  _layer_block(mods, layer), _layer_block(d_skip, layer), _layer_block(ssd_norm, layer),
                  _layer_block(w_attn_o, layer, resident=True),
                  _layer_block(w_ssd_o, layer, resident=True),
                  _layer_block(w_out, layer, resident=True)],
        out_specs=[row(D_MODEL), row(D_MODEL)],
    )
    return pl.pallas_call(
        functools.partial(_comb_kernel, n_ctx_tiles=nct),
        grid_spec=grid_spec,
        out_shape=[jax.ShapeDtypeStruct((n_tok, D_MODEL), F32),
                   jax.ShapeDtypeStruct((n_tok, D_MODEL), BF16)],
        compiler_params=_params(1),
        name="combine",
    )(tabs["crow_comb"], attn_c, attn_l, yf, yb, xbc, z, ga, gs, x, mods, d_skip, ssd_norm, w_attn_o, w_ssd_o,
      w_out)


FFN_COL_CHUNK = D_FF


def _ffn_kernel(crow_ref, hprev_ref, hnext_ref,
                h_ref, hp_ref, hn_ref, x_ref, mod_ref, cw_ref, cb_ref, wu_ref, wd_ref,
                o_ref, sv_ref, sg_ref):
    i = pl.program_id(0)
    tm = h_ref.shape[0]
    halo = hp_ref.shape[0]
    mod = mod_ref[pl.ds(crow_ref[i], 1), :]
    gate2 = mod[:, 5 * D_MODEL:6 * D_MODEL]
    hp = (hp_ref[...].astype(F32) * hprev_ref[i].astype(F32)).astype(BF16)
    hn = (hn_ref[...].astype(F32) * hnext_ref[i].astype(F32)).astype(BF16)
    h_ext = jnp.concatenate([hp, h_ref[...], hn], axis=0)
    acc = jnp.zeros((tm, D_MODEL), F32)
    cc = FFN_COL_CHUNK
    for c in range(D_FF // cc):
        halves = []
        for s_ref, base in ((sv_ref, c * cc), (sg_ref, D_FF + c * cc)):
            s_ref[...] = jnp.dot(h_ext, wu_ref[:, base:base + cc], preferred_element_type=F32)
            u = (cb_ref[:, base:base + cc]
                 + cw_ref[0:1, base:base + cc] * s_ref[halo - 1:halo - 1 + tm, :]
                 + cw_ref[1:2, base:base + cc] * s_ref[halo:halo + tm, :]
                 + cw_ref[2:3, base:base + cc] * s_ref[halo + 1:halo + 1 + tm, :])
            halves.append(u)
        act = (_silu(halves[1]) * halves[0]).astype(BF16)
        acc = acc + jnp.dot(act, wd_ref[c * cc:(c + 1) * cc, :], preferred_element_type=F32)
    o_ref[...] = x_ref[...] + gate2 * acc


def _ffn_call(layer, tabs, h2, x1, mods, ffn_conv_w, ffn_conv_b, w_up, w_down):
    n_tok = x1.shape[0]
    tm = ROW_TILE
    halo = SUBLANES_BF16
    hb = tm // halo
    last_hb = n_tok // halo - 1

    grid_spec = pltpu.PrefetchScalarGridSpec(
        num_scalar_prefetch=3,
        grid=(n_tok // tm,),
        in_specs=[pl.BlockSpec((tm, D_MODEL), lambda i, *_: (i, 0)),
                  pl.BlockSpec((halo, D_MODEL), lambda i, *_: (jnp.maximum(i * hb - 1, 0), 0)),
                  pl.BlockSpec((halo, D_MODEL), lambda i, *_: (jnp.minimum((i + 1) * hb, last_hb), 0)),
                  pl.BlockSpec((tm, D_MODEL), lambda i, *_: (i, 0)),
                  _layer_block(mods, layer), _layer_block(ffn_conv_w, layer), _layer_block(ffn_conv_b, layer),
                  _layer_block(w_up, layer, resident=True),
                  _layer_block(w_down, layer, resident=True)],
        out_specs=pl.BlockSpec((tm, D_MODEL), lambda i, *_: (i, 0)),
        scratch_shapes=[pltpu.VMEM((tm + 2 * halo, FFN_COL_CHUNK), F32),
                        pltpu.VMEM((tm + 2 * halo, FFN_COL_CHUNK), F32)],
    )
    return pl.pallas_call(
        _ffn_kernel,
        grid_spec=grid_spec,
        out_shape=jax.ShapeDtypeStruct((n_tok, D_MODEL), F32),
        compiler_params=_params(1),
        name="conv_ffn",
    )(tabs["crow"], tabs["hprev"], tabs["hnext"], h2, h2, h2, x1, mods, ffn_conv_w, ffn_conv_b, w_up, w_down)


def _tables(batch, seq, dec_batch, dec_seq):
    tm = ROW_TILE
    crow, hprev, hnext, rblk = [], [], [], []
    for b in range(batch):
        for t in range(seq // tm):
            crow.append(0)
            hprev.append(int(t > 0))
            hnext.append(int(t < seq // tm - 1))
            rblk.append(0)
    for b in range(dec_batch):
        for t in range(dec_seq // tm):
            crow.append(1 + b)
            hprev.append(int(t > 0))
            hnext.append(int(t < dec_seq // tm - 1))
            rblk.append(1 + t)
    q = SSD_CHUNKS_PER_STEP * SSM_CHUNK
    assert seq % q == 0 and dec_seq % q == 0
    seqs = [(False, b, seq // q) for b in range(batch)] + [(True, b, dec_seq // q) for b in range(dec_batch)]
    s_chunk_f, s_chunk_b, s_first, s_last, s_islat, s_latb, s_ctxb = [], [], [], [], [], [], []
    base = 0
    for is_lat, b, n in seqs:
        for c in range(n):
            s_chunk_f.append(base + c)
            s_chunk_b.append(base + n - 1 - c)
            s_first.append(int(c == 0))
            s_last.append(int(c == n - 1))
            s_islat.append(int(is_lat))
            s_latb.append(b if is_lat else 0)
            s_ctxb.append(batch - 1 if is_lat else b)
        base += n
    per = COMB_ROW_TILE // tm
    assert len(crow) % per == 0 and all(len(set(crow[i:i + per])) == 1 for i in range(0, len(crow), per))
    crow_comb = crow[::per]
    tabs = dict(crow=crow, crow_comb=crow_comb, hprev=hprev, hnext=hnext, rblk=rblk, s_chunk_f=s_chunk_f,
                s_chunk_b=s_chunk_b,
                s_first=s_first, s_last=s_last, s_islat=s_islat, s_latb=s_latb, s_ctxb=s_ctxb)
    return {k: jnp.asarray(np.asarray(v, np.int32)) for k, v in tabs.items()}


def _rope_tables(dec_seq):
    rows = dec_seq // GRID_W
    row = jnp.repeat(jnp.arange(rows), GRID_W).astype(F32)
    col = jnp.tile(jnp.arange(GRID_W), rows).astype(F32)
    half = HEAD_DIM // 2
    inv_freq = ROPE_THETA ** (-jnp.arange(0, half, 2, dtype=F32) / half)
    ang_r = row[:, None] * inv_freq
    ang_c = col[:, None] * inv_freq
    ang = jnp.concatenate([ang_r, ang_r, ang_c, ang_c], axis=-1)
    sign = jnp.where((jnp.arange(HEAD_DIM) & (HEAD_DIM // 4)) == 0, -1.0, 1.0).astype(F32)
    cos = jnp.concatenate([jnp.ones((ROW_TILE, HEAD_DIM), F32), jnp.cos(ang)], axis=0)
    sin = jnp.concatenate([jnp.zeros((ROW_TILE, HEAD_DIM), F32), jnp.sin(ang) * sign], axis=0)
    return cos, sin


def _dir_lanes(p):
    tail = jnp.zeros(p.shape[:-1] + (LANES - DT_COPIES * SSM_HEADS,), p.dtype)
    return jnp.concatenate([p] * DT_COPIES + [tail], axis=-1)


def _cast_kernel(w_ref, o_ref):
    o_ref[...] = w_ref[...].astype(BF16)


def _cast_call(w, n_cols, tn):
    depth, k, _ = w.shape
    assert n_cols % tn == 0 and tn % LANES == 0
    return pl.pallas_call(
        _cast_kernel,
        grid=(depth, n_cols // tn),
        in_specs=[pl.BlockSpec((1, k, tn), lambda l, j: (l, 0, j))],
        out_specs=pl.BlockSpec((1, k, tn), lambda l, j: (l, 0, j)),
        out_shape=jax.ShapeDtypeStruct((depth, k, n_cols), BF16),
        compiler_params=_params(2),
        name="cast_bf16",
    )(w)


def _cast_t_kernel(w_ref, o_ref):
    o_ref[0] = w_ref[0].T.astype(BF16)


def _cast_t_call(w_t, n_cols, tn):
    depth, _, k = w_t.shape
    assert n_cols % tn == 0 and tn % LANES == 0
    return pl.pallas_call(
        _cast_t_kernel,
        grid=(depth, n_cols // tn),
        in_specs=[pl.BlockSpec((1, tn, k), lambda l, j: (l, j, 0))],
        out_specs=pl.BlockSpec((1, k, tn), lambda l, j: (l, 0, j)),
        out_shape=jax.ShapeDtypeStruct((depth, k, n_cols), BF16),
        compiler_params=_params(2),
        name="cast_t_bf16",
    )(w_t)


def _split_w_in(w_in):
    depth = w_in.shape[0]
    w_t = jnp.swapaxes(w_in, 1, 2)
    w_dt = jnp.swapaxes(w_t[:, DT0:DT0 + 2 * SSM_HEADS, :], 1, 2).reshape(depth, D_MODEL, 2, SSM_HEADS)
    w_dt = _dir_lanes(w_dt).reshape(depth, D_MODEL, 2 * LANES)
    w_gates = _cast_t_call(w_t[:, DT0 + 2 * SSM_HEADS:, :], 2 * D_MODEL, D_MODEL)
    return _cast_t_call(w_t, DT0, DT0 // 4), w_dt.astype(BF16), w_gates


def kernel(x_prompt, x_sample, c, cache_k, cache_v, state_ssd, c_ctx, w_mod, b_mod, w_in, q_norm, k_norm,
           conv_w, conv_b, dt_bias, a_log, d_skip, ssd_norm, w_attn_o, w_ssd_o, w_out, w_up,
           ffn_conv_w, ffn_conv_b, w_down):
    batch, seq, _ = x_prompt.shape
    dec_batch, dec_seq, _ = x_sample.shape
    depth = w_in.shape[0]
    past = cache_k.shape[2]
    n_ctx = batch * seq
    assert seq % ROW_TILE == 0 and dec_seq % ROW_TILE == 0 and n_ctx % dec_seq == 0
    assert 1 + dec_batch <= SUBLANES_F32

    tabs = _tables(batch, seq, dec_batch, dec_seq)
    rope_cos, rope_sin = _rope_tables(dec_seq)
    cond = jnp.concatenate([c_ctx[None, :], c, jnp.zeros((SUBLANES_F32 - 1 - dec_batch, D_MODEL), F32)], axis=0)
    mods = _mod_call(cond, w_mod, b_mod)

    w_main, w_dt, w_gates = _split_w_in(w_in)
    wa = _cast_call(w_attn_o, D_MODEL, D_MODEL)
    ws = _cast_call(w_ssd_o, D_MODEL, D_MODEL // 2)
    wo = _cast_call(w_out, D_MODEL, D_MODEL)
    wu = _cast_call(w_up, 2 * D_FF, D_FF // 2)
    wd = _cast_call(w_down, D_MODEL, D_MODEL // 2)
    qn = q_norm.reshape(depth, 1, HEAD_DIM)
    kn = k_norm.reshape(depth, 1, HEAD_DIM)
    cvb = conv_b.reshape(depth, 1, CONV_CH)
    fcb = ffn_conv_b.reshape(depth, 1, 2 * D_FF)
    dtb = _dir_lanes(dt_bias).reshape(depth, 1, 2 * LANES)
    alog = _dir_lanes(a_log).reshape(depth, 2, 1, LANES)
    dsk = jnp.repeat(d_skip, SSM_HEAD_DIM, axis=-1).reshape(depth, 1, D_INNER)
    nrm = ssd_norm.reshape(depth, 1, D_INNER)
    ck = cache_k.reshape(dec_batch, depth, past, KV_W)
    cv = cache_v.reshape(dec_batch, depth, past, KV_W)

    x = jnp.concatenate([x_prompt.reshape(n_ctx, D_MODEL), x_sample.reshape(dec_batch * dec_seq, D_MODEL)], axis=0)
    new_k = jnp.zeros((batch, depth, seq, H_KV, HEAD_DIM), F32)
    new_v = jnp.zeros((batch, depth, seq, H_KV, HEAD_DIM), F32)
    new_s = jnp.zeros((batch, depth, 2, SSM_HEADS * SSM_HEAD_DIM, SSM_STATE), F32)
    for l in range(depth):
        q, k, v, kf, vf, z, xbc, dt, ga, gs = _in_call(
            l, x, mods, tabs, rope_cos, rope_sin, qn, kn, conv_w, cvb, dtb, w_main, w_dt, w_gates)
        attn_c, new_k, new_v = _attn_ctx_call(l, q, k, v, kf, vf, new_k, new_v, batch, seq)
        attn_l = _attn_lat_call(q, k, v, ck, cv, n_ctx, dec_batch, dec_seq, l)
        yf, yb, new_s = _ssd_call(l, xbc, dt, alog, state_ssd, tabs, new_s)
        x1, h2 = _comb_call(l, tabs, attn_c, attn_l, yf, yb, xbc, z, ga, gs, x, mods, dsk, nrm, wa, ws, wo)
        x = _ffn_call(l, tabs, h2, x1, mods, ffn_conv_w, fcb, wu, wd)
    y_prompt = x[:n_ctx].reshape(batch, seq, D_MODEL)
    y_sample = x[n_ctx:].reshape(dec_batch, dec_seq, D_MODEL)
    new_state = new_s.reshape(batch, depth, 2, SSM_HEADS, SSM_HEAD_DIM, SSM_STATE)
    return (y_prompt, y_sample, new_k, new_v, new_state)
```

```python
import functools
import math

import numpy as np
import jax
import jax.numpy as jnp
from jax import lax
from jax.experimental import pallas as pl
from jax.experimental.pallas import tpu as pltpu

F32 = jnp.float32
BF16 = jnp.bfloat16

D_MODEL = 1024
GRID_W = 64
EPS = 1e-6
H_Q = 8
H_KV = 2
HEAD_DIM = 128
ATTN_W = H_Q * HEAD_DIM
KV_W = H_KV * HEAD_DIM
ROPE_THETA = 10000.0
D_INNER = 2 * D_MODEL
SSM_HEAD_DIM = 64
SSM_HEADS = D_INNER // SSM_HEAD_DIM
SSM_GROUPS = 4
SSM_STATE = 128
SSM_CHUNK = 128
CONV_CH = D_INNER + 2 * SSM_GROUPS * SSM_STATE
D_FF = 2816
GROUP_W = (SSM_HEADS // SSM_GROUPS) * SSM_HEAD_DIM
LOG2E = 1.4426950408889634

LANES = 128
SUBLANES_F32 = 8
SUBLANES_BF16 = 16
VMEM_LIMIT_BYTES = 56 * 1024 * 1024

Q0 = 0
K0 = Q0 + ATTN_W
V0 = K0 + KV_W
Z0 = V0 + KV_W
X0 = Z0 + D_INNER
DT0 = X0 + CONV_CH
DT_COPIES = 3

ROW_TILE = 256
SSD_CHUNKS_PER_STEP = 2
ATTN_Q_TILE = 512


def _sigmoid(x):
    return 0.5 * jnp.tanh(0.5 * x) + 0.5


def _silu(x):
    h = 0.5 * x
    return h * jnp.tanh(h) + h


def _params(n_axes):
    return pltpu.CompilerParams(dimension_semantics=("arbitrary",) * n_axes,
                                vmem_limit_bytes=VMEM_LIMIT_BYTES)


def _resident(shape, index_map):
    return pl.BlockSpec(shape, index_map, pipeline_mode=pl.Buffered(1))


def _layer_block(arr, layer, resident=False):
    shape = (None,) + tuple(arr.shape[1:])
    nz = len(arr.shape) - 1

    def index_map(*_):
        return (layer,) + (0,) * nz

    return _resident(shape, index_map) if resident else pl.BlockSpec(shape, index_map)


def _mod_kernel(cond_ref, w_ref, b_ref, o_ref):
    a = _silu(cond_ref[...]).astype(BF16)
    o_ref[0] = jnp.dot(a, w_ref[0].astype(BF16), preferred_element_type=F32) + b_ref[0]


def _mod_call(cond, w_mod, b_mod):
    depth, _, n = w_mod.shape
    tn = 1536
    return pl.pallas_call(
        _mod_kernel,
        grid=(depth, n // tn),
        in_specs=[pl.BlockSpec(cond.shape, lambda l, j: (0, 0)),
                  pl.BlockSpec((1, D_MODEL, tn), lambda l, j: (l, 0, j)),
                  pl.BlockSpec((1, 1, tn), lambda l, j: (l, 0, j))],
        out_specs=pl.BlockSpec((1, cond.shape[0], tn), lambda l, j: (l, 0, j)),
        out_shape=jax.ShapeDtypeStruct((depth, cond.shape[0], n), F32),
        compiler_params=_params(2),
        name="mod_vectors",
    )(cond, w_mod, b_mod.reshape(depth, 1, n))


def _modulate(x, shift, scale):
    ms = jnp.mean(x * x, axis=-1, keepdims=True)
    return (x * lax.rsqrt(ms + EPS)) * (1.0 + scale) + shift


def _in_kernel(crow_ref, hprev_ref, hnext_ref, rblk_ref, *refs, n_ctx_tiles, split_x):
    del rblk_ref
    i = pl.program_id(0)
    if split_x:
        xc_ref, xcp_ref, xcn_ref, xl_ref, xlp_ref, xln_ref = refs[:6]
        refs = refs[6:]
        is_ctx = i < n_ctx_tiles
        x = jnp.where(is_ctx, xc_ref[...], xl_ref[...])
        xp = jnp.where(is_ctx, xcp_ref[...], xlp_ref[...])
        xn = jnp.where(is_ctx, xcn_ref[...], xln_ref[...])
    else:
        x, xp, xn = refs[0][...], refs[1][...], refs[2][...]
        refs = refs[3:]
    (mod_ref, cos_ref, sin_ref, qn_ref, kn_ref, cw_ref, cb_ref, dtb_ref, w_ref, wdt_ref, wg_ref,
     q_ref, k_ref, v_ref, kf_ref, vf_ref, z_ref, xbc_ref, dt_ref, ga_ref, gs_ref, s_ref) = refs
    tm = x.shape[0]
    halo = xp.shape[0]
    mod = mod_ref[pl.ds(crow_ref[i], 1), :]
    shift = mod[:, 0:D_MODEL]
    scale = mod[:, D_MODEL:2 * D_MODEL]
    hm = _modulate(x, shift, scale)
    hp = _modulate(xp, shift, scale) * hprev_ref[i].astype(F32)
    hn = _modulate(xn, shift, scale) * hnext_ref[i].astype(F32)
    h = hm.astype(BF16)
    h_ext = jnp.concatenate([hp, hm, hn], axis=0).astype(BF16)

    qkv = jnp.dot(h, w_ref[:, Q0:Z0], preferred_element_type=F32)
    cos = cos_ref[...]
    sin = sin_ref[...]
    lane = lax.broadcasted_iota(jnp.int32, (tm, HEAD_DIM), 1)
    first_quarter = (lane & (HEAD_DIM // 4)) == 0

    def norm_rope(u, w):
        un = u * lax.rsqrt(jnp.mean(u * u, axis=-1, keepdims=True) + EPS) * w
        rot = jnp.where(first_quarter, pltpu.roll(un, HEAD_DIM - HEAD_DIM // 4, 1),
                        pltpu.roll(un, HEAD_DIM // 4, 1))
        return un * cos + rot * sin

    q_scale = HEAD_DIM ** -0.5 * LOG2E
    for j in range(H_Q):
        qh = norm_rope(qkv[:, j * HEAD_DIM:(j + 1) * HEAD_DIM], qn_ref[...])
        q_ref[:, j * HEAD_DIM:(j + 1) * HEAD_DIM] = (qh * q_scale).astype(BF16)
    ones = jnp.ones((tm, HEAD_DIM), BF16)
    for j in range(H_KV):
        kh = norm_rope(qkv[:, K0 + j * HEAD_DIM:K0 + (j + 1) * HEAD_DIM], kn_ref[...])
        k_ref[:, j * HEAD_DIM:(j + 1) * HEAD_DIM] = kh.astype(BF16)
        kf_ref[:, j * HEAD_DIM:(j + 1) * HEAD_DIM] = kh
        v_ref[:, 2 * j * HEAD_DIM:(2 * j + 1) * HEAD_DIM] = qkv[:, V0 + j * HEAD_DIM:V0 + (j + 1) * HEAD_DIM].astype(BF16)
        v_ref[:, (2 * j + 1) * HEAD_DIM:(2 * j + 2) * HEAD_DIM] = ones
    vf_ref[...] = qkv[:, V0:Z0]

    z_ref[...] = _silu(jnp.dot(h, w_ref[:, Z0:X0], preferred_element_type=F32)).astype(BF16)

    s_ref[...] = jnp.dot(h_ext, w_ref[:, X0:DT0], preferred_element_type=F32)
    xc = (cb_ref[...] + cw_ref[0:1, :] * s_ref[halo - 1:halo - 1 + tm, :]
          + cw_ref[1:2, :] * s_ref[halo:halo + tm, :]
          + cw_ref[2:3, :] * s_ref[halo + 1:halo + 1 + tm, :])
    xbc_ref[...] = _silu(xc).astype(BF16)

    dt_raw = jnp.dot(h, wdt_ref[...], preferred_element_type=F32) + dtb_ref[...]
    dt_ref[...] = jnp.maximum(dt_raw, 0.0) + jnp.log1p(jnp.exp(-jnp.abs(dt_raw)))

    gates = _sigmoid(jnp.dot(h, wg_ref[...], preferred_element_type=F32)).astype(BF16)
    ga_ref[...] = gates[:, 0:D_MODEL]
    gs_ref[...] = gates[:, D_MODEL:2 * D_MODEL]


def _in_call(layer, x, mods, tabs, rope_cos, rope_sin, q_norm, k_norm, conv_w, conv_b, dt_bias, w_main,
             w_dt, w_gates):
    split_x = isinstance(x, tuple)
    tm = ROW_TILE
    halo = SUBLANES_F32
    hb = tm // halo
    if split_x:
        x_ctx, x_lat = x
        n_ctx, n_lat = x_ctx.shape[0], x_lat.shape[0]
        n_tok = n_ctx + n_lat
        nct = n_ctx // tm
        last_c = n_ctx // halo - 1
        last_l = n_lat // halo - 1
        x_specs = [
            pl.BlockSpec((tm, D_MODEL), lambda i, *_: (jnp.minimum(i, nct - 1), 0)),
            pl.BlockSpec((halo, D_MODEL), lambda i, *_: (jnp.clip(i * hb - 1, 0, last_c), 0)),
            pl.BlockSpec((halo, D_MODEL), lambda i, *_: (jnp.minimum((i + 1) * hb, last_c), 0)),
            pl.BlockSpec((tm, D_MODEL), lambda i, *_: (jnp.maximum(i - nct, 0), 0)),
            pl.BlockSpec((halo, D_MODEL), lambda i, *_: (jnp.maximum((i - nct) * hb - 1, 0), 0)),
            pl.BlockSpec((halo, D_MODEL), lambda i, *_: (jnp.clip((i - nct + 1) * hb, 0, last_l), 0)),
        ]
        x_args = [x_ctx, x_ctx, x_ctx, x_lat, x_lat, x_lat]
    else:
        n_tok = x.shape[0]
        nct = 0
        last_hb = n_tok // halo - 1
        x_specs = [
            pl.BlockSpec((tm, D_MODEL), lambda i, *_: (i, 0)),
            pl.BlockSpec((halo, D_MODEL), lambda i, *_: (jnp.maximum(i * hb - 1, 0), 0)),
            pl.BlockSpec((halo, D_MODEL), lambda i, *_: (jnp.minimum((i + 1) * hb, last_hb), 0)),
        ]
        x_args = [x, x, x]
    n_tiles = n_tok // tm

    def row(width):
        return pl.BlockSpec((tm, width), lambda i, *_: (i, 0))

    grid_spec = pltpu.PrefetchScalarGridSpec(
        num_scalar_prefetch=4,
        grid=(n_tiles,),
        in_specs=x_specs + [
            _layer_block(mods, layer),
            pl.BlockSpec((tm, HEAD_DIM), lambda i, c, p, n, r: (r[i], 0)),
            pl.BlockSpec((tm, HEAD_DIM), lambda i, c, p, n, r: (r[i], 0)),
            _layer_block(q_norm, layer), _layer_block(k_norm, layer),
            _layer_block(conv_w, layer), _layer_block(conv_b, layer), _layer_block(dt_bias, layer),
            _layer_block(w_main, layer, resident=True),
            _layer_block(w_dt, layer, resident=True),
            _layer_block(w_gates, layer, resident=True),
        ],
        out_specs=[row(ATTN_W), row(KV_W), row(2 * KV_W), row(KV_W), row(KV_W), row(D_INNER),
                   row(CONV_CH), row(2 * LANES), row(D_MODEL), row(D_MODEL)],
        scratch_shapes=[pltpu.VMEM((tm + 2 * halo, CONV_CH), F32)],
    )
    sds = jax.ShapeDtypeStruct
    return pl.pallas_call(
        functools.partial(_in_kernel, n_ctx_tiles=nct, split_x=split_x),
        grid_spec=grid_spec,
        out_shape=[sds((n_tok, ATTN_W), BF16), sds((n_tok, KV_W), BF16), sds((n_tok, 2 * KV_W), BF16),
                   sds((n_tok, KV_W), F32), sds((n_tok, KV_W), F32), sds((n_tok, D_INNER), BF16),
                   sds((n_tok, CONV_CH), BF16), sds((n_tok, 2 * LANES), F32),
                   sds((n_tok, D_MODEL), BF16), sds((n_tok, D_MODEL), BF16)],
        compiler_params=_params(1),
        name="in_proj",
    )(tabs["crow"], tabs["hprev"], tabs["hnext"], tabs["rblk"],
      *x_args, mods, rope_cos, rope_sin, q_norm, k_norm, conv_w, conv_b, dt_bias, w_main, w_dt, w_gates)


def _attn_kernel(*refs, has_cache):
    if has_cache:
        q_ref, k_ref, v_ref, kc_ref, vc_ref, o_ref, kt_ref, vcx_ref = refs
        past = kc_ref.shape[0]
    else:
        q_ref, k_ref, v_ref, kf_ref, vf_ref, _, _, o_ref, nk_ref, nv_ref, kt_ref = refs
        past = 0
        for g in range(H_KV):
            nk_ref[:, g, :] = kf_ref[:, g * HEAD_DIM:(g + 1) * HEAD_DIM]
            nv_ref[:, g, :] = vf_ref[:, g * HEAD_DIM:(g + 1) * HEAD_DIM]
    rep = H_Q // H_KV

    @pl.when(pl.program_id(1) == 0)
    def _():
        if has_cache:
            kt_ref[:, 0:past] = kc_ref[...].T.astype(BF16)
            for g in range(H_KV):
                vcx_ref[:, 2 * g * HEAD_DIM:(2 * g + 1) * HEAD_DIM] = vc_ref[:, g * HEAD_DIM:(g + 1) * HEAD_DIM].astype(BF16)
                vcx_ref[:, (2 * g + 1) * HEAD_DIM:(2 * g + 2) * HEAD_DIM] = jnp.ones((past, HEAD_DIM), BF16)
        kt_ref[:, past:] = k_ref[...].astype(F32).T.astype(BF16)

    for j in range(H_Q):
        g = j // rep
        q = q_ref[:, j * HEAD_DIM:(j + 1) * HEAD_DIM]
        s = jnp.dot(q, kt_ref[g * HEAD_DIM:(g + 1) * HEAD_DIM, :], preferred_element_type=F32)
        p = jnp.exp2(s - jnp.max(s, axis=-1, keepdims=True)).astype(BF16)
        o = jnp.dot(p[:, past:], v_ref[:, 2 * g * HEAD_DIM:(2 * g + 2) * HEAD_DIM], preferred_element_type=F32)
        if has_cache:
            o = o + jnp.dot(p[:, 0:past], vcx_ref[:, 2 * g * HEAD_DIM:(2 * g + 2) * HEAD_DIM],
                            preferred_element_type=F32)
        o_ref[:, j * HEAD_DIM:(j + 1) * HEAD_DIM] = (o[:, 0:HEAD_DIM] / o[:, HEAD_DIM:2 * HEAD_DIM]).astype(BF16)


def _attn_ctx_call(layer, q, k, v, kf, vf, new_k, new_v, batch, seq):
    n_tok = batch * seq
    stacked = pl.BlockSpec((None, None, seq, H_KV, HEAD_DIM), lambda b, t: (b, layer, 0, 0, 0))
    return pl.pallas_call(
        functools.partial(_attn_kernel, has_cache=False),
        grid=(batch, 1),
        in_specs=[pl.BlockSpec((seq, ATTN_W), lambda b, t: (b, 0)),
                  pl.BlockSpec((seq, KV_W), lambda b, t: (b, 0)),
                  pl.BlockSpec((seq, 2 * KV_W), lambda b, t: (b, 0)),
                  pl.BlockSpec((seq, KV_W), lambda b, t: (b, 0)),
                  pl.BlockSpec((seq, KV_W), lambda b, t: (b, 0)),
                  pl.BlockSpec(memory_space=pl.ANY),
                  pl.BlockSpec(memory_space=pl.ANY)],
        out_specs=[pl.BlockSpec((seq, ATTN_W), lambda b, t: (b, 0)), stacked, stacked],
        out_shape=[jax.ShapeDtypeStruct((n_tok, ATTN_W), BF16),
                   jax.ShapeDtypeStruct(new_k.shape, F32), jax.ShapeDtypeStruct(new_v.shape, F32)],
        scratch_shapes=[pltpu.VMEM((KV_W, seq), BF16)],
        input_output_aliases={5: 1, 6: 2},
        compiler_params=_params(2),
        name="attn_context",
    )(q, k, v, kf, vf, new_k, new_v)


def _attn_lat_call(q, k, v, cache_k, cache_v, n_ctx_tok, dec_batch, dec_seq, layer):
    tq = ATTN_Q_TILE
    t0 = n_ctx_tok // tq
    s0 = n_ctx_tok // dec_seq
    tiles = dec_seq // tq
    past = cache_k.shape[2]
    return pl.pallas_call(
        functools.partial(_attn_kernel, has_cache=True),
        grid=(dec_batch, tiles),
        in_specs=[pl.BlockSpec((tq, ATTN_W), lambda b, t: (t0 + b * tiles + t, 0)),
                  pl.BlockSpec((dec_seq, KV_W), lambda b, t: (s0 + b, 0)),
                  pl.BlockSpec((dec_seq, 2 * KV_W), lambda b, t: (s0 + b, 0)),
                  pl.BlockSpec((None, None, past, KV_W), lambda b, t: (b, layer, 0, 0)),
                  pl.BlockSpec((None, None, past, KV_W), lambda b, t: (b, layer, 0, 0))],
        out_specs=pl.BlockSpec((tq, ATTN_W), lambda b, t: (b * tiles + t, 0)),
        out_shape=jax.ShapeDtypeStruct((dec_batch * dec_seq, ATTN_W), BF16),
        scratch_shapes=[pltpu.VMEM((KV_W, past + dec_seq), BF16),
                        pltpu.VMEM((past, 2 * KV_W), BF16)],
        compiler_params=_params(2),
        name="attn_latent",
    )(q, k, v, cache_k, cache_v)


def _bf16_split3(x):
    c1 = x.astype(BF16).astype(F32)
    r = x - c1
    c2 = r.astype(BF16).astype(F32)
    c3 = (r - c2).astype(BF16).astype(F32)
    return c1, c2, c3


def _ssd_direction(x_ref, dt_ref, alog, h_ref, y_ref, ecol_ref, ew_ref, d, r0):
    q = SSM_CHUNK
    heads_per_group = SSM_HEADS // SSM_GROUPS
    dt = dt_ref[r0:r0 + q, :]
    a = -jnp.exp(alog)
    row = lax.broadcasted_iota(jnp.int32, (q, q), 0)
    col = lax.broadcasted_iota(jnp.int32, (q, q), 1)
    mask = (row >= col) if d == 0 else (row <= col)
    ones_tri = jnp.where(mask, 1.0, 0.0).astype(BF16)
    lane = lax.broadcasted_iota(jnp.int32, (q, LANES), 1)

    def pick_split(parts):
        return jnp.where(lane < SSM_HEADS, parts[0],
                         jnp.where(lane < 2 * SSM_HEADS, parts[1],
                                   jnp.where(lane < 3 * SSM_HEADS, parts[2], 0.0))).astype(BF16)

    da = jnp.concatenate(_bf16_split3(dt * a), axis=1).astype(BF16)
    cs = jnp.dot(ones_tri, da, preferred_element_type=F32)
    cum = (cs[:, 0:LANES] + cs[:, LANES:2 * LANES] + cs[:, 2 * LANES:3 * LANES]) * LOG2E
    cum_t = cum.T
    dt_t = dt.T
    total = cum[q - 1:q, :] if d == 0 else cum[0:1, :]
    state_w = dt * jnp.exp2(total - cum)
    cum_split = pick_split(_bf16_split3(cum))
    sw_split = pick_split(_bf16_split3(state_w))
    lo = lane < SSM_HEAD_DIM
    lo_row = lo[0:1, :]
    last = q - 1 if d == 0 else 0
    row2 = lax.broadcasted_iota(jnp.int32, (2 * q, LANES), 0)
    lane2 = lax.broadcasted_iota(jnp.int32, (2 * q, LANES), 1)
    pair_mask = jnp.where((row2 < q) == (lane2 < SSM_HEAD_DIM), 1.0, 0.0).astype(BF16)

    for g in range(SSM_GROUPS):
        b0 = D_INNER + g * SSM_STATE
        c0 = D_INNER + SSM_GROUPS * SSM_STATE + g * SSM_STATE
        bg_t = x_ref[r0:r0 + q, b0:b0 + SSM_STATE].astype(F32).T.astype(BF16)
        cg = x_ref[r0:r0 + q, c0:c0 + SSM_STATE]
        cb = jnp.dot(cg, bg_t, preferred_element_type=F32)
        h_t = h_ref[d, g]
        col_g = jnp.dot(cum_split, ecol_ref[:, g * heads_per_group * LANES:(g + 1) * heads_per_group * LANES],
                        preferred_element_type=F32)
        sw_g = jnp.dot(sw_split, ew_ref[:, g * GROUP_W:(g + 1) * GROUP_W], preferred_element_type=F32)
        y_off = jnp.dot(cg, h_t.astype(BF16), preferred_element_type=F32)
        xw_parts = []
        dec_parts = []
        for pp in range(heads_per_group // 2):
            h0 = g * heads_per_group + 2 * pp
            h1 = h0 + 1
            col0 = col_g[:, 2 * pp * LANES:(2 * pp + 1) * LANES]
            col1 = col_g[:, (2 * pp + 1) * LANES:(2 * pp + 2) * LANES]
            w0 = cb * jnp.exp2(jnp.where(mask, col0 - cum_t[h0:h0 + 1, :], -jnp.inf)) * dt_t[h0:h0 + 1, :]
            w1 = cb * jnp.exp2(jnp.where(mask, col1 - cum_t[h1:h1 + 1, :], -jnp.inf)) * dt_t[h1:h1 + 1, :]
            x0 = h0 * SSM_HEAD_DIM
            xpair = x_ref[r0:r0 + q, x0:x0 + LANES]
            w01 = jnp.concatenate([w0.astype(BF16), w1.astype(BF16)], axis=1)
            x01 = jnp.concatenate([xpair, xpair], axis=0) * pair_mask
            y_diag = jnp.dot(w01, x01, preferred_element_type=F32)
            colsel = jnp.where(lo, col0, col1)
            y_pair = y_diag + y_off[:, pp * LANES:(pp + 1) * LANES] * jnp.exp2(colsel)
            y_ref[r0:r0 + q, x0:x0 + LANES] = y_pair.astype(BF16)
            xw_parts.append((xpair.astype(F32) * sw_g[:, pp * LANES:(pp + 1) * LANES]).astype(BF16))
            dec_parts.append(jnp.exp2(jnp.where(lo_row, col0[last:last + 1, :], col1[last:last + 1, :])))
        xw = jnp.concatenate(xw_parts, axis=1)
        decay = jnp.concatenate(dec_parts, axis=1)
        h_ref[d, g] = h_t * decay + jnp.dot(bg_t, xw, preferred_element_type=F32)


def _ssd_kernel(chf_ref, chb_ref, first_ref, last_ref, islat_ref, latb_ref, ctxb_ref,
                xf_ref, xb_ref, dtf_ref, dtb_ref, alog_ref, init_ref, ecol_ref, ew_ref,
                yf_ref, yb_ref, fin_ref, h_ref):
    del chf_ref, chb_ref, latb_ref, ctxb_ref
    step = pl.program_id(0)

    @pl.when((first_ref[step] == 1) & (islat_ref[step] == 0))
    def _():
        h_ref[...] = jnp.zeros(h_ref.shape, F32)

    @pl.when((first_ref[step] == 1) & (islat_ref[step] == 1))
    def _():
        for d in range(2):
            for g in range(SSM_GROUPS):
                h_ref[d, g] = init_ref[d, g * GROUP_W:(g + 1) * GROUP_W, :].T

    for sub in range(SSD_CHUNKS_PER_STEP):
        _ssd_direction(xf_ref, dtf_ref, alog_ref[0], h_ref, yf_ref, ecol_ref, ew_ref, 0, sub * SSM_CHUNK)
        _ssd_direction(xb_ref, dtb_ref, alog_ref[1], h_ref, yb_ref, ecol_ref, ew_ref, 1,
                       (SSD_CHUNKS_PER_STEP - 1 - sub) * SSM_CHUNK)

    @pl.when((last_ref[step] == 1) & (islat_ref[step] == 0))
    def _():
        for d in range(2):
            for g in range(SSM_GROUPS):
                fin_ref[d, g * GROUP_W:(g + 1) * GROUP_W, :] = h_ref[d, g].T


def _expansion_matrices():
    r = np.arange(LANES)
    head = r % SSM_HEADS
    live = r < DT_COPIES * SSM_HEADS
    ecol = (live[:, None] & (head[:, None] == (np.arange(SSM_HEADS * LANES) // LANES)[None, :]))
    ew = (live[:, None] & (head[:, None] == (np.arange(D_INNER) // SSM_HEAD_DIM)[None, :]))
    return jnp.asarray(ecol, BF16), jnp.asarray(ew, BF16)


def _ssd_call(layer, xbc, dt, a_log, state_ssd, tabs, new_s):
    n_tok = xbc.shape[0]
    q = SSD_CHUNKS_PER_STEP * SSM_CHUNK
    n_steps = n_tok // q
    hpn = SSM_HEADS * SSM_HEAD_DIM
    dec_batch, depth = state_ssd.shape[0], state_ssd.shape[1]
    init = state_ssd.reshape(dec_batch, depth, 2, hpn, SSM_STATE)
    ecol, ew = _expansion_matrices()

    in_specs = [
        pl.BlockSpec((q, CONV_CH), lambda j, cf, cb, *_: (cf[j], 0)),
        pl.BlockSpec((q, CONV_CH), lambda j, cf, cb, *_: (cb[j], 0)),
        pl.BlockSpec((q, LANES), lambda j, cf, cb, *_: (cf[j], 0)),
        pl.BlockSpec((q, LANES), lambda j, cf, cb, *_: (cb[j], 1)),
        _layer_block(a_log, layer),
        pl.BlockSpec((None, None, 2, hpn, SSM_STATE),
                     lambda j, cf, cb, fi, la, il, lb, xb: (lb[j], layer, 0, 0, 0)),
        _resident(ecol.shape, lambda j, *_: (0, 0)),
        _resident(ew.shape, lambda j, *_: (0, 0)),
    ]
    in_specs.append(pl.BlockSpec(memory_space=pl.ANY))
    args = [xbc, xbc, dt, dt, a_log, init, ecol, ew, new_s]
    n_prefetch = 7
    grid_spec = pltpu.PrefetchScalarGridSpec(
        num_scalar_prefetch=n_prefetch,
        grid=(n_steps,),
        in_specs=in_specs,
        out_specs=[
            pl.BlockSpec((q, D_INNER), lambda j, cf, cb, *_: (cf[j], 0)),
            pl.BlockSpec((q, D_INNER), lambda j, cf, cb, *_: (cb[j], 0)),
            pl.BlockSpec((None, None, 2, hpn, SSM_STATE),
                         lambda j, cf, cb, fi, la, il, lb, xb: (xb[j], layer, 0, 0, 0)),
        ],
        scratch_shapes=[pltpu.VMEM((2, SSM_GROUPS, SSM_STATE, GROUP_W), F32)],
    )

    alias_in = n_prefetch + len(args) - 1

    def body(*refs):
        _ssd_kernel(*(refs[:alias_in] + refs[alias_in + 1:]))

    return pl.pallas_call(
        body,
        grid_spec=grid_spec,
        out_shape=[jax.ShapeDtypeStruct((n_tok, D_INNER), BF16),
                   jax.ShapeDtypeStruct((n_tok, D_INNER), BF16),
                   jax.ShapeDtypeStruct(new_s.shape, F32)],
        input_output_aliases={alias_in: 2},
        compiler_params=_params(1),
        name="ssd_scan",
    )(tabs["s_chunk_f"], tabs["s_chunk_b"], tabs["s_first"], tabs["s_last"], tabs["s_islat"],
      tabs["s_latb"], tabs["s_ctxb"], *args)


COMB_ROW_TILE = 512


def _comb_kernel(crow_ref, attn_c_ref, attn_l_ref, yf_ref, yb_ref, xs_ref, z_ref, ga_ref, gs_ref, *refs,
                 n_ctx_tiles, split_x):
    i = pl.program_id(0)
    if split_x:
        x = jnp.where(i < n_ctx_tiles, refs[0][...], refs[1][...])
        refs = refs[2:]
    else:
        x = refs[0][...]
        refs = refs[1:]
    mod_ref, dskip_ref, nw_ref, wa_ref, ws_ref, wo_ref, x1_ref, h2_ref = refs
    mod = mod_ref[pl.ds(crow_ref[i], 1), :]
    gate1 = mod[:, 2 * D_MODEL:3 * D_MODEL]
    shift2 = mod[:, 3 * D_MODEL:4 * D_MODEL]
    scale2 = mod[:, 4 * D_MODEL:5 * D_MODEL]
    attn = jnp.where(i < n_ctx_tiles, attn_c_ref[...], attn_l_ref[...])
    y = yf_ref[...].astype(F32) + yb_ref[...].astype(F32) + dskip_ref[...] * xs_ref[...].astype(F32)
    y = y * z_ref[...].astype(F32)
    yn = (y * lax.rsqrt(jnp.mean(y * y, axis=-1, keepdims=True) + EPS)) * nw_ref[...]
    merged = (ga_ref[...].astype(F32) * jnp.dot(attn, wa_ref[...], preferred_element_type=F32)
              + gs_ref[...].astype(F32) * jnp.dot(yn.astype(BF16), ws_ref[...], preferred_element_type=F32))
    x1 = x + gate1 * jnp.dot(merged.astype(BF16), wo_ref[...], preferred_element_type=F32)
    x1_ref[...] = x1
    h2_ref[...] = _modulate(x1, shift2, scale2).astype(BF16)


def _comb_call(layer, tabs, attn_c, attn_l, yf, yb, xbc, z, ga, gs, x, mods, d_skip, ssd_norm, w_attn_o,
               w_ssd_o, w_out):
    split_x = isinstance(x, tuple)
    n_tok = attn_c.shape[0] + attn_l.shape[0]
    tm = COMB_ROW_TILE
    nct = attn_c.shape[0] // tm
    assert attn_c.shape[0] % tm == 0 and attn_l.shape[0] % tm == 0

    def row(width):
        return pl.BlockSpec((tm, width), lambda i, *_: (i, 0))

    if split_x:
        x_specs = [pl.BlockSpec((tm, D_MODEL), lambda i, *_: (jnp.minimum(i, nct - 1), 0)),
                   pl.BlockSpec((tm, D_MODEL), lambda i, *_: (jnp.maximum(i - nct, 0), 0))]
        x_args = list(x)
    else:
        x_specs = [row(D_MODEL)]
        x_args = [x]

    grid_spec = pltpu.PrefetchScalarGridSpec(
        num_scalar_prefetch=1,
        grid=(n_tok // tm,),
        in_specs=[pl.BlockSpec((tm, ATTN_W), lambda i, *_: (jnp.minimum(i, nct - 1), 0)),
                  pl.BlockSpec((tm, ATTN_W), lambda i, *_: (jnp.maximum(i - nct, 0), 0)),
                  row(D_INNER), row(D_INNER), row(D_INNER), row(D_INNER),
                  row(D_MODEL), row(D_MODEL)] + x_specs + [
                  _layer_block(mods, layer), _layer_block(d_skip, layer), _layer_block(ssd_norm, layer),
                  _layer_block(w_attn_o, layer, resident=True),
                  _layer_block(w_ssd_o, layer, resident=True),
                  _layer_block(w_out, layer, resident=True)],
        out_specs=[row(D_MODEL), row(D_MODEL)],
    )
    return pl.pallas_call(
        functools.partial(_comb_kernel, n_ctx_tiles=nct, split_x=split_x),
        grid_spec=grid_spec,
        out_shape=[jax.ShapeDtypeStruct((n_tok, D_MODEL), F32),
                   jax.ShapeDtypeStruct((n_tok, D_MODEL), BF16)],
        compiler_params=_params(1),
        name="combine",
    )(tabs["crow_comb"], attn_c, attn_l, yf, yb, xbc, z, ga, gs, *x_args, mods, d_skip, ssd_norm, w_attn_o,
      w_ssd_o, w_out)


def _ffn_kernel(crow_ref, hprev_ref, hnext_ref,
                h_ref, hp_ref, hn_ref, x_ref, mod_ref, cw_ref, cb_ref, wu_ref, wd_ref,
                *refs, n_ctx_tiles, split_out):
    i = pl.program_id(0)
    s_ref = refs[-1]
    tm = h_ref.shape[0]
    halo = hp_ref.shape[0]
    mod = mod_ref[pl.ds(crow_ref[i], 1), :]
    gate2 = mod[:, 5 * D_MODEL:6 * D_MODEL]
    hp = (hp_ref[...].astype(F32) * hprev_ref[i].astype(F32)).astype(BF16)
    hn = (hn_ref[...].astype(F32) * hnext_ref[i].astype(F32)).astype(BF16)
    h_ext = jnp.concatenate([hp, h_ref[...], hn], axis=0)
    s_ref[...] = jnp.dot(h_ext, wu_ref[...], preferred_element_type=F32)
    u = (cb_ref[...] + cw_ref[0:1, :] * s_ref[halo - 1:halo - 1 + tm, :]
         + cw_ref[1:2, :] * s_ref[halo:halo + tm, :]
         + cw_ref[2:3, :] * s_ref[halo + 1:halo + 1 + tm, :])
    act = (_silu(u[:, D_FF:2 * D_FF]) * u[:, 0:D_FF]).astype(BF16)
    out = x_ref[...] + gate2 * jnp.dot(act, wd_ref[...], preferred_element_type=F32)

    if not split_out:
        refs[0][...] = out
        return
    oc_ref, ol_ref = refs[0], refs[1]

    @pl.when(i < n_ctx_tiles)
    def _():
        oc_ref[...] = out

    @pl.when(i >= n_ctx_tiles)
    def _():
        ol_ref[...] = out


def _ffn_call(layer, tabs, h2, x1, mods, ffn_conv_w, ffn_conv_b, w_up, w_down, n_ctx, split_out):
    n_tok = x1.shape[0]
    tm = ROW_TILE
    nct = n_ctx // tm
    halo = SUBLANES_BF16
    hb = tm // halo
    last_hb = n_tok // halo - 1
    if split_out:
        out_specs = [pl.BlockSpec((tm, D_MODEL), lambda i, *_: (jnp.minimum(i, nct - 1), 0)),
                     pl.BlockSpec((tm, D_MODEL), lambda i, *_: (jnp.maximum(i - nct, 0), 0))]
        out_shape = [jax.ShapeDtypeStruct((n_ctx, D_MODEL), F32),
                     jax.ShapeDtypeStruct((n_tok - n_ctx, D_MODEL), F32)]
    else:
        out_specs = [pl.BlockSpec((tm, D_MODEL), lambda i, *_: (i, 0))]
        out_shape = [jax.ShapeDtypeStruct((n_tok, D_MODEL), F32)]

    grid_spec = pltpu.PrefetchScalarGridSpec(
        num_scalar_prefetch=3,
        grid=(n_tok // tm,),
        in_specs=[pl.BlockSpec((tm, D_MODEL), lambda i, *_: (i, 0)),
                  pl.BlockSpec((halo, D_MODEL), lambda i, *_: (jnp.maximum(i * hb - 1, 0), 0)),
                  pl.BlockSpec((halo, D_MODEL), lambda i, *_: (jnp.minimum((i + 1) * hb, last_hb), 0)),
                  pl.BlockSpec((tm, D_MODEL), lambda i, *_: (i, 0)),
                  _layer_block(mods, layer), _layer_block(ffn_conv_w, layer), _layer_block(ffn_conv_b, layer),
                  _layer_block(w_up, layer, resident=True),
                  _layer_block(w_down, layer, resident=True)],
        out_specs=out_specs,
        scratch_shapes=[pltpu.VMEM((tm + 2 * halo, 2 * D_FF), F32)],
    )
    outs = pl.pallas_call(
        functools.partial(_ffn_kernel, n_ctx_tiles=nct, split_out=split_out),
        grid_spec=grid_spec,
        out_shape=out_shape,
        compiler_params=_params(1),
        name="conv_ffn",
    )(tabs["crow"], tabs["hprev"], tabs["hnext"], h2, h2, h2, x1, mods, ffn_conv_w, ffn_conv_b, w_up, w_down)
    return tuple(outs) if split_out else outs[0]


def _tables(batch, seq, dec_batch, dec_seq):
    tm = ROW_TILE
    crow, hprev, hnext, rblk = [], [], [], []
    for b in range(batch):
        for t in range(seq // tm):
            crow.append(0)
            hprev.append(int(t > 0))
            hnext.append(int(t < seq // tm - 1))
            rblk.append(0)
    for b in range(dec_batch):
        for t in range(dec_seq // tm):
            crow.append(1 + b)
            hprev.append(int(t > 0))
            hnext.append(int(t < dec_seq // tm - 1))
            rblk.append(1 + t)
    q = SSD_CHUNKS_PER_STEP * SSM_CHUNK
    assert seq % q == 0 and dec_seq % q == 0
    seqs = [(False, b, seq // q) for b in range(batch)] + [(True, b, dec_seq // q) for b in range(dec_batch)]
    s_chunk_f, s_chunk_b, s_first, s_last, s_islat, s_latb, s_ctxb = [], [], [], [], [], [], []
    base = 0
    for is_lat, b, n in seqs:
        for c in range(n):
            s_chunk_f.append(base + c)
            s_chunk_b.append(base + n - 1 - c)
            s_first.append(int(c == 0))
            s_last.append(int(c == n - 1))
            s_islat.append(int(is_lat))
            s_latb.append(b if is_lat else 0)
            s_ctxb.append(batch - 1 if is_lat else b)
        base += n
    per = COMB_ROW_TILE // tm
    assert len(crow) % per == 0 and all(len(set(crow[i:i + per])) == 1 for i in range(0, len(crow), per))
    crow_comb = crow[::per]
    tabs = dict(crow=crow, crow_comb=crow_comb, hprev=hprev, hnext=hnext, rblk=rblk, s_chunk_f=s_chunk_f,
                s_chunk_b=s_chunk_b,
                s_first=s_first, s_last=s_last, s_islat=s_islat, s_latb=s_latb, s_ctxb=s_ctxb)
    return {k: jnp.asarray(np.asarray(v, np.int32)) for k, v in tabs.items()}


def _rope_tables(dec_seq):
    rows = dec_seq // GRID_W
    row = jnp.repeat(jnp.arange(rows), GRID_W).astype(F32)
    col = jnp.tile(jnp.arange(GRID_W), rows).astype(F32)
    half = HEAD_DIM // 2
    inv_freq = ROPE_THETA ** (-jnp.arange(0, half, 2, dtype=F32) / half)
    ang_r = row[:, None] * inv_freq
    ang_c = col[:, None] * inv_freq
    ang = jnp.concatenate([ang_r, ang_r, ang_c, ang_c], axis=-1)
    sign = jnp.where((jnp.arange(HEAD_DIM) & (HEAD_DIM // 4)) == 0, -1.0, 1.0).astype(F32)
    cos = jnp.concatenate([jnp.ones((ROW_TILE, HEAD_DIM), F32), jnp.cos(ang)], axis=0)
    sin = jnp.concatenate([jnp.zeros((ROW_TILE, HEAD_DIM), F32), jnp.sin(ang) * sign], axis=0)
    return cos, sin


def _dir_lanes(p):
    tail = jnp.zeros(p.shape[:-1] + (LANES - DT_COPIES * SSM_HEADS,), p.dtype)
    return jnp.concatenate([p] * DT_COPIES + [tail], axis=-1)


def _cast_kernel(w_ref, o_ref):
    o_ref[...] = w_ref[...].astype(BF16)


def _cast_call(w, n_cols, tn):
    depth, k, _ = w.shape
    assert n_cols % tn == 0 and tn % LANES == 0
    return pl.pallas_call(
        _cast_kernel,
        grid=(depth, n_cols // tn),
        in_specs=[pl.BlockSpec((1, k, tn), lambda l, j: (l, 0, j))],
        out_specs=pl.BlockSpec((1, k, tn), lambda l, j: (l, 0, j)),
        out_shape=jax.ShapeDtypeStruct((depth, k, n_cols), BF16),
        compiler_params=_params(2),
        name="cast_bf16",
    )(w)


def _cast_t_kernel(w_ref, o_ref):
    o_ref[0] = w_ref[0].T.astype(BF16)


def _cast_t_call(w_t, n_cols, tn):
    depth, _, k = w_t.shape
    assert n_cols % tn == 0 and tn % LANES == 0
    return pl.pallas_call(
        _cast_t_kernel,
        grid=(depth, n_cols // tn),
        in_specs=[pl.BlockSpec((1, tn, k), lambda l, j: (l, j, 0))],
        out_specs=pl.BlockSpec((1, k, tn), lambda l, j: (l, 0, j)),
        out_shape=jax.ShapeDtypeStruct((depth, k, n_cols), BF16),
        compiler_params=_params(2),
        name="cast_t_bf16",
    )(w_t)


def _split_w_in(w_in):
    depth = w_in.shape[0]
    w_t = jnp.swapaxes(w_in, 1, 2)
    w_dt = jnp.swapaxes(w_t[:, DT0:DT0 + 2 * SSM_HEADS, :], 1, 2).reshape(depth, D_MODEL, 2, SSM_HEADS)
    w_dt = _dir_lanes(w_dt).reshape(depth, D_MODEL, 2 * LANES)
    w_gates = _cast_t_call(w_t[:, DT0 + 2 * SSM_HEADS:, :], 2 * D_MODEL, D_MODEL)
    return _cast_t_call(w_t, DT0, DT0 // 4), w_dt.astype(BF16), w_gates


def kernel(x_prompt, x_sample, c, cache_k, cache_v, state_ssd, c_ctx, w_mod, b_mod, w_in, q_norm, k_norm,
           conv_w, conv_b, dt_bias, a_log, d_skip, ssd_norm, w_attn_o, w_ssd_o, w_out, w_up,
           ffn_conv_w, ffn_conv_b, w_down):
    batch, seq, _ = x_prompt.shape
    dec_batch, dec_seq, _ = x_sample.shape
    depth = w_in.shape[0]
    past = cache_k.shape[2]
    n_ctx = batch * seq
    assert seq % ROW_TILE == 0 and dec_seq % ROW_TILE == 0 and n_ctx % dec_seq == 0
    assert 1 + dec_batch <= SUBLANES_F32

    tabs = _tables(batch, seq, dec_batch, dec_seq)
    rope_cos, rope_sin = _rope_tables(dec_seq)
    cond = jnp.concatenate([c_ctx[None, :], c, jnp.zeros((SUBLANES_F32 - 1 - dec_batch, D_MODEL), F32)], axis=0)
    mods = _mod_call(cond, w_mod, b_mod)

    w_main, w_dt, w_gates = _split_w_in(w_in)
    wa = _cast_call(w_attn_o, D_MODEL, D_MODEL)
    ws = _cast_call(w_ssd_o, D_MODEL, D_MODEL // 2)
    wo = _cast_call(w_out, D_MODEL, D_MODEL)
    wu = _cast_call(w_up, 2 * D_FF, D_FF // 2)
    wd = _cast_call(w_down, D_MODEL, D_MODEL // 2)
    qn = q_norm.reshape(depth, 1, HEAD_DIM)
    kn = k_norm.reshape(depth, 1, HEAD_DIM)
    cvb = conv_b.reshape(depth, 1, CONV_CH)
    fcb = ffn_conv_b.reshape(depth, 1, 2 * D_FF)
    dtb = _dir_lanes(dt_bias).reshape(depth, 1, 2 * LANES)
    alog = _dir_lanes(a_log).reshape(depth, 2, 1, LANES)
    dsk = jnp.repeat(d_skip, SSM_HEAD_DIM, axis=-1).reshape(depth, 1, D_INNER)
    nrm = ssd_norm.reshape(depth, 1, D_INNER)
    ck = cache_k.reshape(dec_batch, depth, past, KV_W)
    cv = cache_v.reshape(dec_batch, depth, past, KV_W)

    x = (x_prompt.reshape(n_ctx, D_MODEL), x_sample.reshape(dec_batch * dec_seq, D_MODEL))
    new_k = jnp.zeros((batch, depth, seq, H_KV, HEAD_DIM), F32)
    new_v = jnp.zeros((batch, depth, seq, H_KV, HEAD_DIM), F32)
    new_s = jnp.zeros((batch, depth, 2, SSM_HEADS * SSM_HEAD_DIM, SSM_STATE), F32)
    for l in range(depth):
        q, k, v, kf, vf, z, xbc, dt, ga, gs = _in_call(
            l, x, mods, tabs, rope_cos, rope_sin, qn, kn, conv_w, cvb, dtb, w_main, w_dt, w_gates)
        attn_c, new_k, new_v = _attn_ctx_call(l, q, k, v, kf, vf, new_k, new_v, batch, seq)
        attn_l = _attn_lat_call(q, k, v, ck, cv, n_ctx, dec_batch, dec_seq, l)
        yf, yb, new_s = _ssd_call(l, xbc, dt, alog, state_ssd, tabs, new_s)
        x1, h2 = _comb_call(l, tabs, attn_c, attn_l, yf, yb, xbc, z, ga, gs, x, mods, dsk, nrm, wa, ws, wo)
        x = _ffn_call(l, tabs, h2, x1, mods, ffn_conv_w, fcb, wu, wd, n_ctx, split_out=(l == depth - 1))
    y_prompt = x[0].reshape(batch, seq, D_MODEL)
    y_sample = x[1].reshape(dec_batch, dec_seq, D_MODEL)
    new_state = new_s.reshape(batch, depth, 2, SSM_HEADS, SSM_HEAD_DIM, SSM_STATE)
    return (y_prompt, y_sample, new_k, new_v, new_state)
```

```python
import functools
import math

import numpy as np
import jax
import jax.numpy as jnp
from jax import lax
from jax.experimental import pallas as pl
from jax.experimental.pallas import tpu as pltpu

F32 = jnp.float32
BF16 = jnp.bfloat16

D_MODEL = 1024
GRID_W = 64
EPS = 1e-6
H_Q = 8
H_KV = 2
HEAD_DIM = 128
ATTN_W = H_Q * HEAD_DIM
KV_W = H_KV * HEAD_DIM
ROPE_THETA = 10000.0
D_INNER = 2 * D_MODEL
SSM_HEAD_DIM = 64
SSM_HEADS = D_INNER // SSM_HEAD_DIM
SSM_GROUPS = 4
SSM_STATE = 128
SSM_CHUNK = 128
CONV_CH = D_INNER + 2 * SSM_GROUPS * SSM_STATE
D_FF = 2816
GROUP_W = (SSM_HEADS // SSM_GROUPS) * SSM_HEAD_DIM
LOG2E = 1.4426950408889634

LANES = 128
SUBLANES_F32 = 8
SUBLANES_BF16 = 16
VMEM_LIMIT_BYTES = 56 * 1024 * 1024

Q0 = 0
K0 = Q0 + ATTN_W
V0 = K0 + KV_W
Z0 = V0 + KV_W
X0 = Z0 + D_INNER
DT0 = X0 + CONV_CH
DT_COPIES = 3

ROW_TILE = 256
SSD_CHUNKS_PER_STEP = 2
ATTN_Q_TILE = 512


def _sigmoid(x):
    return 0.5 * jnp.tanh(0.5 * x) + 0.5


def _silu(x):
    h = 0.5 * x
    return h * jnp.tanh(h) + h


def _params(n_axes):
    return pltpu.CompilerParams(dimension_semantics=("arbitrary",) * n_axes,
                                vmem_limit_bytes=VMEM_LIMIT_BYTES)


def _resident(shape, index_map):
    return pl.BlockSpec(shape, index_map, pipeline_mode=pl.Buffered(1))


def _layer_block(arr, layer, resident=False):
    shape = (None,) + tuple(arr.shape[1:])
    nz = len(arr.shape) - 1

    def index_map(*_):
        return (layer,) + (0,) * nz

    return _resident(shape, index_map) if resident else pl.BlockSpec(shape, index_map)


def _mod_kernel(cond_ref, w_ref, b_ref, o_ref):
    a = _silu(cond_ref[...]).astype(BF16)
    o_ref[0] = jnp.dot(a, w_ref[0].astype(BF16), preferred_element_type=F32) + b_ref[0]


def _mod_call(cond, w_mod, b_mod):
    depth, _, n = w_mod.shape
    tn = 1536
    return pl.pallas_call(
        _mod_kernel,
        grid=(depth, n // tn),
        in_specs=[pl.BlockSpec(cond.shape, lambda l, j: (0, 0)),
                  pl.BlockSpec((1, D_MODEL, tn), lambda l, j: (l, 0, j)),
                  pl.BlockSpec((1, 1, tn), lambda l, j: (l, 0, j))],
        out_specs=pl.BlockSpec((1, cond.shape[0], tn), lambda l, j: (l, 0, j)),
        out_shape=jax.ShapeDtypeStruct((depth, cond.shape[0], n), F32),
        compiler_params=_params(2),
        name="mod_vectors",
    )(cond, w_mod, b_mod.reshape(depth, 1, n))


def _modulate(x, shift, scale):
    ms = jnp.mean(x * x, axis=-1, keepdims=True)
    return (x * lax.rsqrt(ms + EPS)) * (1.0 + scale) + shift


def _in_kernel(crow_ref, hprev_ref, hnext_ref, rblk_ref, *refs, n_ctx_tiles, split_x):
    del rblk_ref
    i = pl.program_id(0)
    if split_x:
        xc_ref, xcp_ref, xcn_ref, xl_ref, xlp_ref, xln_ref = refs[:6]
        refs = refs[6:]
        is_ctx = i < n_ctx_tiles
        x = jnp.where(is_ctx, xc_ref[...], xl_ref[...])
        xp = jnp.where(is_ctx, xcp_ref[...], xlp_ref[...])
        xn = jnp.where(is_ctx, xcn_ref[...], xln_ref[...])
    else:
        x, xp, xn = refs[0][...], refs[1][...], refs[2][...]
        refs = refs[3:]
    (mod_ref, cos_ref, sin_ref, qn_ref, kn_ref, cw_ref, cb_ref, dtb_ref, w_ref, wdt_ref, wg_ref,
     q_ref, k_ref, v_ref, kf_ref, vf_ref, z_ref, xbc_ref, dt_ref, ga_ref, gs_ref, s_ref) = refs
    tm = x.shape[0]
    halo = xp.shape[0]
    mod = mod_ref[pl.ds(crow_ref[i], 1), :]
    shift = mod[:, 0:D_MODEL]
    scale = mod[:, D_MODEL:2 * D_MODEL]
    hm = _modulate(x, shift, scale)
    hp = _modulate(xp, shift, scale) * hprev_ref[i].astype(F32)
    hn = _modulate(xn, shift, scale) * hnext_ref[i].astype(F32)
    h = hm.astype(BF16)
    h_ext = jnp.concatenate([hp, hm, hn], axis=0).astype(BF16)

    qkv = jnp.dot(h, w_ref[:, Q0:Z0], preferred_element_type=F32)
    cos = cos_ref[...]
    sin = sin_ref[...]
    lane = lax.broadcasted_iota(jnp.int32, (tm, HEAD_DIM), 1)
    first_quarter = (lane & (HEAD_DIM // 4)) == 0

    def norm_rope(u, w):
        un = u * lax.rsqrt(jnp.mean(u * u, axis=-1, keepdims=True) + EPS) * w
        rot = jnp.where(first_quarter, pltpu.roll(un, HEAD_DIM - HEAD_DIM // 4, 1),
                        pltpu.roll(un, HEAD_DIM // 4, 1))
        return un * cos + rot * sin

    q_scale = HEAD_DIM ** -0.5 * LOG2E
    for j in range(H_Q):
        qh = norm_rope(qkv[:, j * HEAD_DIM:(j + 1) * HEAD_DIM], qn_ref[...])
        q_ref[:, j * HEAD_DIM:(j + 1) * HEAD_DIM] = (qh * q_scale).astype(BF16)
    ones = jnp.ones((tm, HEAD_DIM), BF16)
    for j in range(H_KV):
        kh = norm_rope(qkv[:, K0 + j * HEAD_DIM:K0 + (j + 1) * HEAD_DIM], kn_ref[...])
        k_ref[:, j * HEAD_DIM:(j + 1) * HEAD_DIM] = kh.astype(BF16)
        kf_ref[:, j * HEAD_DIM:(j + 1) * HEAD_DIM] = kh
        v_ref[:, 2 * j * HEAD_DIM:(2 * j + 1) * HEAD_DIM] = qkv[:, V0 + j * HEAD_DIM:V0 + (j + 1) * HEAD_DIM].astype(BF16)
        v_ref[:, (2 * j + 1) * HEAD_DIM:(2 * j + 2) * HEAD_DIM] = ones
    vf_ref[...] = qkv[:, V0:Z0]

    z_ref[...] = _silu(jnp.dot(h, w_ref[:, Z0:X0], preferred_element_type=F32)).astype(BF16)

    s_ref[...] = jnp.dot(h_ext, w_ref[:, X0:DT0], preferred_element_type=F32)
    xc = (cb_ref[...] + cw_ref[0:1, :] * s_ref[halo - 1:halo - 1 + tm, :]
          + cw_ref[1:2, :] * s_ref[halo:halo + tm, :]
          + cw_ref[2:3, :] * s_ref[halo + 1:halo + 1 + tm, :])
    xbc_ref[...] = _silu(xc).astype(BF16)

    dt_raw = jnp.dot(h, wdt_ref[...], preferred_element_type=F32) + dtb_ref[...]
    dt_ref[...] = jnp.maximum(dt_raw, 0.0) + jnp.log1p(jnp.exp(-jnp.abs(dt_raw)))

    gates = _sigmoid(jnp.dot(h, wg_ref[...], preferred_element_type=F32)).astype(BF16)
    ga_ref[...] = gates[:, 0:D_MODEL]
    gs_ref[...] = gates[:, D_MODEL:2 * D_MODEL]


def _in_call(layer, x, mods, tabs, rope_cos, rope_sin, q_norm, k_norm, conv_w, conv_b, dt_bias, w_main,
             w_dt, w_gates):
    split_x = isinstance(x, tuple)
    tm = ROW_TILE
    halo = SUBLANES_F32
    hb = tm // halo
    if split_x:
        x_ctx, x_lat = x
        n_ctx, n_lat = x_ctx.shape[0], x_lat.shape[0]
        n_tok = n_ctx + n_lat
        nct = n_ctx // tm
        last_c = n_ctx // halo - 1
        last_l = n_lat // halo - 1
        x_specs = [
            pl.BlockSpec((tm, D_MODEL), lambda i, *_: (jnp.minimum(i, nct - 1), 0)),
            pl.BlockSpec((halo, D_MODEL), lambda i, *_: (jnp.clip(i * hb - 1, 0, last_c), 0)),
            pl.BlockSpec((halo, D_MODEL), lambda i, *_: (jnp.minimum((i + 1) * hb, last_c), 0)),
            pl.BlockSpec((tm, D_MODEL), lambda i, *_: (jnp.maximum(i - nct, 0), 0)),
            pl.BlockSpec((halo, D_MODEL), lambda i, *_: (jnp.maximum((i - nct) * hb - 1, 0), 0)),
            pl.BlockSpec((halo, D_MODEL), lambda i, *_: (jnp.clip((i - nct + 1) * hb, 0, last_l), 0)),
        ]
        x_args = [x_ctx, x_ctx, x_ctx, x_lat, x_lat, x_lat]
    else:
        n_tok = x.shape[0]
        nct = 0
        last_hb = n_tok // halo - 1
        x_specs = [
            pl.BlockSpec((tm, D_MODEL), lambda i, *_: (i, 0)),
            pl.BlockSpec((halo, D_MODEL), lambda i, *_: (jnp.maximum(i * hb - 1, 0), 0)),
            pl.BlockSpec((halo, D_MODEL), lambda i, *_: (jnp.minimum((i + 1) * hb, last_hb), 0)),
        ]
        x_args = [x, x, x]
    n_tiles = n_tok // tm

    def row(width):
        return pl.BlockSpec((tm, width), lambda i, *_: (i, 0))

    grid_spec = pltpu.PrefetchScalarGridSpec(
        num_scalar_prefetch=4,
        grid=(n_tiles,),
        in_specs=x_specs + [
            _layer_block(mods, layer),
            pl.BlockSpec((tm, HEAD_DIM), lambda i, c, p, n, r: (r[i], 0)),
            pl.BlockSpec((tm, HEAD_DIM), lambda i, c, p, n, r: (r[i], 0)),
            _layer_block(q_norm, layer), _layer_block(k_norm, layer),
            _layer_block(conv_w, layer), _layer_block(conv_b, layer), _layer_block(dt_bias, layer),
            _layer_block(w_main, layer, resident=True),
            _layer_block(w_dt, layer, resident=True),
            _layer_block(w_gates, layer, resident=True),
        ],
        out_specs=[row(ATTN_W), row(KV_W), row(2 * KV_W), row(KV_W), row(KV_W), row(D_INNER),
                   row(CONV_CH), row(2 * LANES), row(D_MODEL), row(D_MODEL)],
        scratch_shapes=[pltpu.VMEM((tm + 2 * halo, CONV_CH), F32)],
    )
    sds = jax.ShapeDtypeStruct
    return pl.pallas_call(
        functools.partial(_in_kernel, n_ctx_tiles=nct, split_x=split_x),
        grid_spec=grid_spec,
        out_shape=[sds((n_tok, ATTN_W), BF16), sds((n_tok, KV_W), BF16), sds((n_tok, 2 * KV_W), BF16),
                   sds((n_tok, KV_W), F32), sds((n_tok, KV_W), F32), sds((n_tok, D_INNER), BF16),
                   sds((n_tok, CONV_CH), BF16), sds((n_tok, 2 * LANES), F32),
                   sds((n_tok, D_MODEL), BF16), sds((n_tok, D_MODEL), BF16)],
        compiler_params=_params(1),
        name="in_proj",
    )(tabs["crow"], tabs["hprev"], tabs["hnext"], tabs["rblk"],
      *x_args, mods, rope_cos, rope_sin, q_norm, k_norm, conv_w, conv_b, dt_bias, w_main, w_dt, w_gates)


def _attn_kernel(*refs, has_cache):
    if has_cache:
        q_ref, k_ref, v_ref, kc_ref, vc_ref, o_ref, kt_ref, vcx_ref = refs
        past = kc_ref.shape[0]
    else:
        q_ref, k_ref, v_ref, kf_ref, vf_ref, _, _, o_ref, nk_ref, nv_ref, kt_ref = refs
        past = 0
        for g in range(H_KV):
            nk_ref[:, g, :] = kf_ref[:, g * HEAD_DIM:(g + 1) * HEAD_DIM]
            nv_ref[:, g, :] = vf_ref[:, g * HEAD_DIM:(g + 1) * HEAD_DIM]
    rep = H_Q // H_KV

    @pl.when(pl.program_id(1) == 0)
    def _():
        if has_cache:
            kt_ref[:, 0:past] = kc_ref[...].T.astype(BF16)
            for g in range(H_KV):
                vcx_ref[:, 2 * g * HEAD_DIM:(2 * g + 1) * HEAD_DIM] = vc_ref[:, g * HEAD_DIM:(g + 1) * HEAD_DIM].astype(BF16)
                vcx_ref[:, (2 * g + 1) * HEAD_DIM:(2 * g + 2) * HEAD_DIM] = jnp.ones((past, HEAD_DIM), BF16)
        kt_ref[:, past:] = k_ref[...].astype(F32).T.astype(BF16)

    for j in range(H_Q):
        g = j // rep
        q = q_ref[:, j * HEAD_DIM:(j + 1) * HEAD_DIM]
        s = jnp.dot(q, kt_ref[g * HEAD_DIM:(g + 1) * HEAD_DIM, :], preferred_element_type=F32)
        p = jnp.exp2(s - jnp.max(s, axis=-1, keepdims=True)).astype(BF16)
        o = jnp.dot(p[:, past:], v_ref[:, 2 * g * HEAD_DIM:(2 * g + 2) * HEAD_DIM], preferred_element_type=F32)
        if has_cache:
            o = o + jnp.dot(p[:, 0:past], vcx_ref[:, 2 * g * HEAD_DIM:(2 * g + 2) * HEAD_DIM],
                            preferred_element_type=F32)
        o_ref[:, j * HEAD_DIM:(j + 1) * HEAD_DIM] = (o[:, 0:HEAD_DIM] / o[:, HEAD_DIM:2 * HEAD_DIM]).astype(BF16)


def _attn_ctx_call(layer, q, k, v, kf, vf, new_k, new_v, batch, seq):
    n_tok = batch * seq
    stacked = pl.BlockSpec((None, None, seq, H_KV, HEAD_DIM), lambda b, t: (b, layer, 0, 0, 0))
    return pl.pallas_call(
        functools.partial(_attn_kernel, has_cache=False),
        grid=(batch, 1),
        in_specs=[pl.BlockSpec((seq, ATTN_W), lambda b, t: (b, 0)),
                  pl.BlockSpec((seq, KV_W), lambda b, t: (b, 0)),
                  pl.BlockSpec((seq, 2 * KV_W), lambda b, t: (b, 0)),
                  pl.BlockSpec((seq, KV_W), lambda b, t: (b, 0)),
                  pl.BlockSpec((seq, KV_W), lambda b, t: (b, 0)),
                  pl.BlockSpec(memory_space=pl.ANY),
                  pl.BlockSpec(memory_space=pl.ANY)],
        out_specs=[pl.BlockSpec((seq, ATTN_W), lambda b, t: (b, 0)), stacked, stacked],
        out_shape=[jax.ShapeDtypeStruct((n_tok, ATTN_W), BF16),
                   jax.ShapeDtypeStruct(new_k.shape, F32), jax.ShapeDtypeStruct(new_v.shape, F32)],
        scratch_shapes=[pltpu.VMEM((KV_W, seq), BF16)],
        input_output_aliases={5: 1, 6: 2},
        compiler_params=_params(2),
        name="attn_context",
    )(q, k, v, kf, vf, new_k, new_v)


def _attn_lat_call(q, k, v, cache_k, cache_v, n_ctx_tok, dec_batch, dec_seq, layer):
    tq = ATTN_Q_TILE
    t0 = n_ctx_tok // tq
    s0 = n_ctx_tok // dec_seq
    tiles = dec_seq // tq
    past = cache_k.shape[2]
    return pl.pallas_call(
        functools.partial(_attn_kernel, has_cache=True),
        grid=(dec_batch, tiles),
        in_specs=[pl.BlockSpec((tq, ATTN_W), lambda b, t: (t0 + b * tiles + t, 0)),
                  pl.BlockSpec((dec_seq, KV_W), lambda b, t: (s0 + b, 0)),
                  pl.BlockSpec((dec_seq, 2 * KV_W), lambda b, t: (s0 + b, 0)),
                  pl.BlockSpec((None, None, past, KV_W), lambda b, t: (b, layer, 0, 0)),
                  pl.BlockSpec((None, None, past, KV_W), lambda b, t: (b, layer, 0, 0))],
        out_specs=pl.BlockSpec((tq, ATTN_W), lambda b, t: (b * tiles + t, 0)),
        out_shape=jax.ShapeDtypeStruct((dec_batch * dec_seq, ATTN_W), BF16),
        scratch_shapes=[pltpu.VMEM((KV_W, past + dec_seq), BF16),
                        pltpu.VMEM((past, 2 * KV_W), BF16)],
        compiler_params=_params(2),
        name="attn_latent",
    )(q, k, v, cache_k, cache_v)


def _bf16_split3(x):
    c1 = x.astype(BF16).astype(F32)
    r = x - c1
    c2 = r.astype(BF16).astype(F32)
    c3 = (r - c2).astype(BF16).astype(F32)
    return c1, c2, c3


def _ssd_direction(x_ref, dt_ref, alog, h_ref, y_ref, ecol_ref, ew_ref, d, r0):
    q = SSM_CHUNK
    heads_per_group = SSM_HEADS // SSM_GROUPS
    dt = dt_ref[r0:r0 + q, :]
    a = -jnp.exp(alog)
    row = lax.broadcasted_iota(jnp.int32, (q, q), 0)
    col = lax.broadcasted_iota(jnp.int32, (q, q), 1)
    mask = (row >= col) if d == 0 else (row <= col)
    ones_tri = jnp.where(mask, 1.0, 0.0).astype(BF16)
    lane = lax.broadcasted_iota(jnp.int32, (q, LANES), 1)

    def pick_split(parts):
        return jnp.where(lane < SSM_HEADS, parts[0],
                         jnp.where(lane < 2 * SSM_HEADS, parts[1],
                                   jnp.where(lane < 3 * SSM_HEADS, parts[2], 0.0))).astype(BF16)

    da = jnp.concatenate(_bf16_split3(dt * a), axis=1).astype(BF16)
    cs = jnp.dot(ones_tri, da, preferred_element_type=F32)
    cum = (cs[:, 0:LANES] + cs[:, LANES:2 * LANES] + cs[:, 2 * LANES:3 * LANES]) * LOG2E
    cum_t = cum.T
    dt_t = dt.T
    total = cum[q - 1:q, :] if d == 0 else cum[0:1, :]
    state_w = dt * jnp.exp2(total - cum)
    cum_split = pick_split(_bf16_split3(cum))
    sw_split = pick_split(_bf16_split3(state_w))
    lo = lane < SSM_HEAD_DIM
    lo_row = lo[0:1, :]
    last = q - 1 if d == 0 else 0
    row2 = lax.broadcasted_iota(jnp.int32, (2 * q, LANES), 0)
    lane2 = lax.broadcasted_iota(jnp.int32, (2 * q, LANES), 1)
    pair_mask = jnp.where((row2 < q) == (lane2 < SSM_HEAD_DIM), 1.0, 0.0).astype(BF16)

    for g in range(SSM_GROUPS):
        b0 = D_INNER + g * SSM_STATE
        c0 = D_INNER + SSM_GROUPS * SSM_STATE + g * SSM_STATE
        bg_t = x_ref[r0:r0 + q, b0:b0 + SSM_STATE].astype(F32).T.astype(BF16)
        cg = x_ref[r0:r0 + q, c0:c0 + SSM_STATE]
        cb = jnp.dot(cg, bg_t, preferred_element_type=F32)
        h_t = h_ref[d, g]
        col_g = jnp.dot(cum_split, ecol_ref[:, g * heads_per_group * LANES:(g + 1) * heads_per_group * LANES],
                        preferred_element_type=F32)
        sw_g = jnp.dot(sw_split, ew_ref[:, g * GROUP_W:(g + 1) * GROUP_W], preferred_element_type=F32)
        y_off = jnp.dot(cg, h_t.astype(BF16), preferred_element_type=F32)
        xw_parts = []
        dec_parts = []
        for pp in range(heads_per_group // 2):
            h0 = g * heads_per_group + 2 * pp
            h1 = h0 + 1
            col0 = col_g[:, 2 * pp * LANES:(2 * pp + 1) * LANES]
            col1 = col_g[:, (2 * pp + 1) * LANES:(2 * pp + 2) * LANES]
            w0 = cb * jnp.exp2(jnp.where(mask, col0 - cum_t[h0:h0 + 1, :], -jnp.inf)) * dt_t[h0:h0 + 1, :]
            w1 = cb * jnp.exp2(jnp.where(mask, col1 - cum_t[h1:h1 + 1, :], -jnp.inf)) * dt_t[h1:h1 + 1, :]
            x0 = h0 * SSM_HEAD_DIM
            xpair = x_ref[r0:r0 + q, x0:x0 + LANES]
            w01 = jnp.concatenate([w0.astype(BF16), w1.astype(BF16)], axis=1)
            x01 = jnp.concatenate([xpair, xpair], axis=0) * pair_mask
            y_diag = jnp.dot(w01, x01, preferred_element_type=F32)
            colsel = jnp.where(lo, col0, col1)
            y_pair = y_diag + y_off[:, pp * LANES:(pp + 1) * LANES] * jnp.exp2(colsel)
            y_ref[r0:r0 + q, x0:x0 + LANES] = y_pair.astype(BF16)
            xw_parts.append((xpair.astype(F32) * sw_g[:, pp * LANES:(pp + 1) * LANES]).astype(BF16))
            dec_parts.append(jnp.exp2(jnp.where(lo_row, col0[last:last + 1, :], col1[last:last + 1, :])))
        xw = jnp.concatenate(xw_parts, axis=1)
        decay = jnp.concatenate(dec_parts, axis=1)
        h_ref[d, g] = h_t * decay + jnp.dot(bg_t, xw, preferred_element_type=F32)


def _ssd_kernel(chf_ref, chb_ref, first_ref, last_ref, islat_ref, latb_ref, ctxb_ref,
                xf_ref, xb_ref, dtf_ref, dtb_ref, alog_ref, init_ref, ecol_ref, ew_ref,
                yf_ref, yb_ref, fin_ref, h_ref):
    del chf_ref, chb_ref, latb_ref, ctxb_ref
    step = pl.program_id(0)

    @pl.when((first_ref[step] == 1) & (islat_ref[step] == 0))
    def _():
        h_ref[...] = jnp.zeros(h_ref.shape, F32)

    @pl.when((first_ref[step] == 1) & (islat_ref[step] == 1))
    def _():
        for d in range(2):
            for g in range(SSM_GROUPS):
                h_ref[d, g] = init_ref[d, g * GROUP_W:(g + 1) * GROUP_W, :].T

    for sub in range(SSD_CHUNKS_PER_STEP):
        _ssd_direction(xf_ref, dtf_ref, alog_ref[0], h_ref, yf_ref, ecol_ref, ew_ref, 0, sub * SSM_CHUNK)
        _ssd_direction(xb_ref, dtb_ref, alog_ref[1], h_ref, yb_ref, ecol_ref, ew_ref, 1,
                       (SSD_CHUNKS_PER_STEP - 1 - sub) * SSM_CHUNK)

    @pl.when((last_ref[step] == 1) & (islat_ref[step] == 0))
    def _():
        for d in range(2):
            for g in range(SSM_GROUPS):
                fin_ref[d, g * GROUP_W:(g + 1) * GROUP_W, :] = h_ref[d, g].T


def _expansion_matrices():
    r = np.arange(LANES)
    head = r % SSM_HEADS
    live = r < DT_COPIES * SSM_HEADS
    ecol = (live[:, None] & (head[:, None] == (np.arange(SSM_HEADS * LANES) // LANES)[None, :]))
    ew = (live[:, None] & (head[:, None] == (np.arange(D_INNER) // SSM_HEAD_DIM)[None, :]))
    return jnp.asarray(ecol, BF16), jnp.asarray(ew, BF16)


def _ssd_call(layer, xbc, dt, a_log, state_ssd, tabs, new_s):
    n_tok = xbc.shape[0]
    q = SSD_CHUNKS_PER_STEP * SSM_CHUNK
    n_steps = n_tok // q
    hpn = SSM_HEADS * SSM_HEAD_DIM
    dec_batch, depth = state_ssd.shape[0], state_ssd.shape[1]
    init = state_ssd.reshape(dec_batch, depth, 2, hpn, SSM_STATE)
    ecol, ew = _expansion_matrices()

    in_specs = [
        pl.BlockSpec((q, CONV_CH), lambda j, cf, cb, *_: (cf[j], 0)),
        pl.BlockSpec((q, CONV_CH), lambda j, cf, cb, *_: (cb[j], 0)),
        pl.BlockSpec((q, LANES), lambda j, cf, cb, *_: (cf[j], 0)),
        pl.BlockSpec((q, LANES), lambda j, cf, cb, *_: (cb[j], 1)),
        _layer_block(a_log, layer),
        pl.BlockSpec((None, None, 2, hpn, SSM_STATE),
                     lambda j, cf, cb, fi, la, il, lb, xb: (lb[j], layer, 0, 0, 0)),
        _resident(ecol.shape, lambda j, *_: (0, 0)),
        _resident(ew.shape, lambda j, *_: (0, 0)),
    ]
    in_specs.append(pl.BlockSpec(memory_space=pl.ANY))
    args = [xbc, xbc, dt, dt, a_log, init, ecol, ew, new_s]
    n_prefetch = 7
    grid_spec = pltpu.PrefetchScalarGridSpec(
        num_scalar_prefetch=n_prefetch,
        grid=(n_steps,),
        in_specs=in_specs,
        out_specs=[
            pl.BlockSpec((q, D_INNER), lambda j, cf, cb, *_: (cf[j], 0)),
            pl.BlockSpec((q, D_INNER), lambda j, cf, cb, *_: (cb[j], 0)),
            pl.BlockSpec((None, None, 2, hpn, SSM_STATE),
                         lambda j, cf, cb, fi, la, il, lb, xb: (xb[j], layer, 0, 0, 0)),
        ],
        scratch_shapes=[pltpu.VMEM((2, SSM_GROUPS, SSM_STATE, GROUP_W), F32)],
    )

    alias_in = n_prefetch + len(args) - 1

    def body(*refs):
        _ssd_kernel(*(refs[:alias_in] + refs[alias_in + 1:]))

    return pl.pallas_call(
        body,
        grid_spec=grid_spec,
        out_shape=[jax.ShapeDtypeStruct((n_tok, D_INNER), BF16),
                   jax.ShapeDtypeStruct((n_tok, D_INNER), BF16),
                   jax.ShapeDtypeStruct(new_s.shape, F32)],
        input_output_aliases={alias_in: 2},
        compiler_params=_params(1),
        name="ssd_scan",
    )(tabs["s_chunk_f"], tabs["s_chunk_b"], tabs["s_first"], tabs["s_last"], tabs["s_islat"],
      tabs["s_latb"], tabs["s_ctxb"], *args)


COMB_ROW_TILE = 512


def _comb_kernel(crow_ref, attn_c_ref, attn_l_ref, yf_ref, yb_ref, xs_ref, z_ref, ga_ref, gs_ref, *refs,
                 n_ctx_tiles, split_x):
    i = pl.program_id(0)
    if split_x:
        x = jnp.where(i < n_ctx_tiles, refs[0][...], refs[1][...])
        refs = refs[2:]
    else:
        x = refs[0][...]
        refs = refs[1:]
    mod_ref, dskip_ref, nw_ref, wa_ref, ws_ref, wo_ref, x1_ref, h2_ref = refs
    mod = mod_ref[pl.ds(crow_ref[i], 1), :]
    gate1 = mod[:, 2 * D_MODEL:3 * D_MODEL]
    shift2 = mod[:, 3 * D_MODEL:4 * D_MODEL]
    scale2 = mod[:, 4 * D_MODEL:5 * D_MODEL]
    attn = jnp.where(i < n_ctx_tiles, attn_c_ref[...], attn_l_ref[...])
    hw = D_INNER // 2
    parts = []
    ssq = jnp.zeros((x.shape[0], 1), F32)
    for c in range(2):
        cs = slice(c * hw, (c + 1) * hw)
        yc = (yf_ref[:, cs].astype(F32) + yb_ref[:, cs].astype(F32)
              + dskip_ref[:, cs] * xs_ref[:, cs].astype(F32)) * z_ref[:, cs].astype(F32)
        ssq = ssq + jnp.sum(yc * yc, axis=-1, keepdims=True)
        parts.append(yc)
    inv = lax.rsqrt(ssq * (1.0 / D_INNER) + EPS)
    yn = jnp.concatenate([((parts[c] * inv) * nw_ref[:, c * hw:(c + 1) * hw]).astype(BF16) for c in range(2)], axis=1)
    merged = (ga_ref[...].astype(F32) * jnp.dot(attn, wa_ref[...], preferred_element_type=F32)
              + gs_ref[...].astype(F32) * jnp.dot(yn, ws_ref[...], preferred_element_type=F32))
    x1 = x + gate1 * jnp.dot(merged.astype(BF16), wo_ref[...], preferred_element_type=F32)
    x1_ref[...] = x1
    h2_ref[...] = _modulate(x1, shift2, scale2).astype(BF16)


def _comb_call(layer, tabs, attn_c, attn_l, yf, yb, xbc, z, ga, gs, x, mods, d_skip, ssd_norm, w_attn_o,
               w_ssd_o, w_out):
    split_x = isinstance(x, tuple)
    n_tok = attn_c.shape[0] + attn_l.shape[0]
    tm = COMB_ROW_TILE
    nct = attn_c.shape[0] // tm
    assert attn_c.shape[0] % tm == 0 and attn_l.shape[0] % tm == 0

    def row(width):
        return pl.BlockSpec((tm, width), lambda i, *_: (i, 0))

    if split_x:
        x_specs = [pl.BlockSpec((tm, D_MODEL), lambda i, *_: (jnp.minimum(i, nct - 1), 0)),
                   pl.BlockSpec((tm, D_MODEL), lambda i, *_: (jnp.maximum(i - nct, 0), 0))]
        x_args = list(x)
    else:
        x_specs = [row(D_MODEL)]
        x_args = [x]

    grid_spec = pltpu.PrefetchScalarGridSpec(
        num_scalar_prefetch=1,
        grid=(n_tok // tm,),
        in_specs=[pl.BlockSpec((tm, ATTN_W), lambda i, *_: (jnp.minimum(i, nct - 1), 0)),
                  pl.BlockSpec((tm, ATTN_W), lambda i, *_: (jnp.maximum(i - nct, 0), 0)),
                  row(D_INNER), row(D_INNER), row(D_INNER), row(D_INNER),
                  row(D_MODEL), row(D_MODEL)] + x_specs + [
                  _layer_block(mods, layer), _layer_block(d_skip, layer), _layer_block(ssd_norm, layer),
                  _layer_block(w_attn_o, layer, resident=True),
                  _layer_block(w_ssd_o, layer, resident=True),
                  _layer_block(w_out, layer, resident=True)],
        out_specs=[row(D_MODEL), row(D_MODEL)],
    )
    return pl.pallas_call(
        functools.partial(_comb_kernel, n_ctx_tiles=nct, split_x=split_x),
        grid_spec=grid_spec,
        out_shape=[jax.ShapeDtypeStruct((n_tok, D_MODEL), F32),
                   jax.ShapeDtypeStruct((n_tok, D_MODEL), BF16)],
        compiler_params=_params(1),
        name="combine",
    )(tabs["crow_comb"], attn_c, attn_l, yf, yb, xbc, z, ga, gs, *x_args, mods, d_skip, ssd_norm, w_attn_o,
      w_ssd_o, w_out)


def _ffn_kernel(crow_ref, hprev_ref, hnext_ref,
                h_ref, hp_ref, hn_ref, x_ref, mod_ref, cw_ref, cb_ref, wu_ref, wd_ref,
                *refs, n_ctx_tiles, split_out):
    i = pl.program_id(0)
    s_ref = refs[-1]
    tm = h_ref.shape[0]
    halo = hp_ref.shape[0]
    mod = mod_ref[pl.ds(crow_ref[i], 1), :]
    gate2 = mod[:, 5 * D_MODEL:6 * D_MODEL]
    hp = (hp_ref[...].astype(F32) * hprev_ref[i].astype(F32)).astype(BF16)
    hn = (hn_ref[...].astype(F32) * hnext_ref[i].astype(F32)).astype(BF16)
    h_ext = jnp.concatenate([hp, h_ref[...], hn], axis=0)
    s_ref[...] = jnp.dot(h_ext, wu_ref[...], preferred_element_type=F32)
    u = (cb_ref[...] + cw_ref[0:1, :] * s_ref[halo - 1:halo - 1 + tm, :]
         + cw_ref[1:2, :] * s_ref[halo:halo + tm, :]
         + cw_ref[2:3, :] * s_ref[halo + 1:halo + 1 + tm, :])
    act = (_silu(u[:, D_FF:2 * D_FF]) * u[:, 0:D_FF]).astype(BF16)
    out = x_ref[...] + gate2 * jnp.dot(act, wd_ref[...], preferred_element_type=F32)

    if not split_out:
        refs[0][...] = out
        return
    oc_ref, ol_ref = refs[0], refs[1]

    @pl.when(i < n_ctx_tiles)
    def _():
        oc_ref[...] = out

    @pl.when(i >= n_ctx_tiles)
    def _():
        ol_ref[...] = out


def _ffn_call(layer, tabs, h2, x1, mods, ffn_conv_w, ffn_conv_b, w_up, w_down, n_ctx, split_out):
    n_tok = x1.shape[0]
    tm = ROW_TILE
    nct = n_ctx // tm
    halo = SUBLANES_BF16
    hb = tm // halo
    last_hb = n_tok // halo - 1
    if split_out:
        out_specs = [pl.BlockSpec((tm, D_MODEL), lambda i, *_: (jnp.minimum(i, nct - 1), 0)),
                     pl.BlockSpec((tm, D_MODEL), lambda i, *_: (jnp.maximum(i - nct, 0), 0))]
        out_shape = [jax.ShapeDtypeStruct((n_ctx, D_MODEL), F32),
                     jax.ShapeDtypeStruct((n_tok - n_ctx, D_MODEL), F32)]
    else:
        out_specs = [pl.BlockSpec((tm, D_MODEL), lambda i, *_: (i, 0))]
        out_shape = [jax.ShapeDtypeStruct((n_tok, D_MODEL), F32)]

    grid_spec = pltpu.PrefetchScalarGridSpec(
        num_scalar_prefetch=3,
        grid=(n_tok // tm,),
        in_specs=[pl.BlockSpec((tm, D_MODEL), lambda i, *_: (i, 0)),
                  pl.BlockSpec((halo, D_MODEL), lambda i, *_: (jnp.maximum(i * hb - 1, 0), 0)),
                  pl.BlockSpec((halo, D_MODEL), lambda i, *_: (jnp.minimum((i + 1) * hb, last_hb), 0)),
                  pl.BlockSpec((tm, D_MODEL), lambda i, *_: (i, 0)),
                  _layer_block(mods, layer), _layer_block(ffn_conv_w, layer), _layer_block(ffn_conv_b, layer),
                  _layer_block(w_up, layer, resident=True),
                  _layer_block(w_down, layer, resident=True)],
        out_specs=out_specs,
        scratch_shapes=[pltpu.VMEM((tm + 2 * halo, 2 * D_FF), F32)],
    )
    outs = pl.pallas_call(
        functools.partial(_ffn_kernel, n_ctx_tiles=nct, split_out=split_out),
        grid_spec=grid_spec,
        out_shape=out_shape,
        compiler_params=_params(1),
        name="conv_ffn",
    )(tabs["crow"], tabs["hprev"], tabs["hnext"], h2, h2, h2, x1, mods, ffn_conv_w, ffn_conv_b, w_up, w_down)
    return tuple(outs) if split_out else outs[0]


def _tables(batch, seq, dec_batch, dec_seq):
    tm = ROW_TILE
    crow, hprev, hnext, rblk = [], [], [], []
    for b in range(batch):
        for t in range(seq // tm):
            crow.append(0)
            hprev.append(int(t > 0))
            hnext.append(int(t < seq // tm - 1))
            rblk.append(0)
    for b in range(dec_batch):
        for t in range(dec_seq // tm):
            crow.append(1 + b)
            hprev.append(int(t > 0))
            hnext.append(int(t < dec_seq // tm - 1))
            rblk.append(1 + t)
    q = SSD_CHUNKS_PER_STEP * SSM_CHUNK
    assert seq % q == 0 and dec_seq % q == 0
    seqs = [(False, b, seq // q) for b in range(batch)] + [(True, b, dec_seq // q) for b in range(dec_batch)]
    s_chunk_f, s_chunk_b, s_first, s_last, s_islat, s_latb, s_ctxb = [], [], [], [], [], [], []
    base = 0
    for is_lat, b, n in seqs:
        for c in range(n):
            s_chunk_f.append(base + c)
            s_chunk_b.append(base + n - 1 - c)
            s_first.append(int(c == 0))
            s_last.append(int(c == n - 1))
            s_islat.append(int(is_lat))
            s_latb.append(b if is_lat else 0)
            s_ctxb.append(batch - 1 if is_lat else b)
        base += n
    per = COMB_ROW_TILE // tm
    assert len(crow) % per == 0 and all(len(set(crow[i:i + per])) == 1 for i in range(0, len(crow), per))
    crow_comb = crow[::per]
    tabs = dict(crow=crow, crow_comb=crow_comb, hprev=hprev, hnext=hnext, rblk=rblk, s_chunk_f=s_chunk_f,
                s_chunk_b=s_chunk_b,
                s_first=s_first, s_last=s_last, s_islat=s_islat, s_latb=s_latb, s_ctxb=s_ctxb)
    return {k: jnp.asarray(np.asarray(v, np.int32)) for k, v in tabs.items()}


def _rope_tables(dec_seq):
    rows = dec_seq // GRID_W
    row = jnp.repeat(jnp.arange(rows), GRID_W).astype(F32)
    col = jnp.tile(jnp.arange(GRID_W), rows).astype(F32)
    half = HEAD_DIM // 2
    inv_freq = ROPE_THETA ** (-jnp.arange(0, half, 2, dtype=F32) / half)
    ang_r = row[:, None] * inv_freq
    ang_c = col[:, None] * inv_freq
    ang = jnp.concatenate([ang_r, ang_r, ang_c, ang_c], axis=-1)
    sign = jnp.where((jnp.arange(HEAD_DIM) & (HEAD_DIM // 4)) == 0, -1.0, 1.0).astype(F32)
    cos = jnp.concatenate([jnp.ones((ROW_TILE, HEAD_DIM), F32), jnp.cos(ang)], axis=0)
    sin = jnp.concatenate([jnp.zeros((ROW_TILE, HEAD_DIM), F32), jnp.sin(ang) * sign], axis=0)
    return cos, sin


def _dir_lanes(p):
    tail = jnp.zeros(p.shape[:-1] + (LANES - DT_COPIES * SSM_HEADS,), p.dtype)
    return jnp.concatenate([p] * DT_COPIES + [tail], axis=-1)


def _cast_kernel(w_ref, o_ref):
    o_ref[...] = w_ref[...].astype(BF16)


def _cast_call(w, n_cols, tn):
    depth, k, _ = w.shape
    assert n_cols % tn == 0 and tn % LANES == 0
    return pl.pallas_call(
        _cast_kernel,
        grid=(depth, n_cols // tn),
        in_specs=[pl.BlockSpec((1, k, tn), lambda l, j: (l, 0, j))],
        out_specs=pl.BlockSpec((1, k, tn), lambda l, j: (l, 0, j)),
        out_shape=jax.ShapeDtypeStruct((depth, k, n_cols), BF16),
        compiler_params=_params(2),
        name="cast_bf16",
    )(w)


def _cast_t_kernel(w_ref, o_ref):
    o_ref[0] = w_ref[0].T.astype(BF16)


def _cast_t_call(w_t, n_cols, tn):
    depth, _, k = w_t.shape
    assert n_cols % tn == 0 and tn % LANES == 0
    return pl.pallas_call(
        _cast_t_kernel,
        grid=(depth, n_cols // tn),
        in_specs=[pl.BlockSpec((1, tn, k), lambda l, j: (l, j, 0))],
        out_specs=pl.BlockSpec((1, k, tn), lambda l, j: (l, 0, j)),
        out_shape=jax.ShapeDtypeStruct((depth, k, n_cols), BF16),
        compiler_params=_params(2),
        name="cast_t_bf16",
    )(w_t)


def _split_w_in(w_in):
    depth = w_in.shape[0]
    w_t = jnp.swapaxes(w_in, 1, 2)
    w_dt = jnp.swapaxes(w_t[:, DT0:DT0 + 2 * SSM_HEADS, :], 1, 2).reshape(depth, D_MODEL, 2, SSM_HEADS)
    w_dt = _dir_lanes(w_dt).reshape(depth, D_MODEL, 2 * LANES)
    w_gates = _cast_t_call(w_t[:, DT0 + 2 * SSM_HEADS:, :], 2 * D_MODEL, D_MODEL)
    return _cast_t_call(w_t, DT0, DT0 // 4), w_dt.astype(BF16), w_gates


def kernel(x_prompt, x_sample, c, cache_k, cache_v, state_ssd, c_ctx, w_mod, b_mod, w_in, q_norm, k_norm,
           conv_w, conv_b, dt_bias, a_log, d_skip, ssd_norm, w_attn_o, w_ssd_o, w_out, w_up,
           ffn_conv_w, ffn_conv_b, w_down):
    batch, seq, _ = x_prompt.shape
    dec_batch, dec_seq, _ = x_sample.shape
    depth = w_in.shape[0]
    past = cache_k.shape[2]
    n_ctx = batch * seq
    assert seq % ROW_TILE == 0 and dec_seq % ROW_TILE == 0 and n_ctx % dec_seq == 0
    assert 1 + dec_batch <= SUBLANES_F32

    tabs = _tables(batch, seq, dec_batch, dec_seq)
    rope_cos, rope_sin = _rope_tables(dec_seq)
    cond = jnp.concatenate([c_ctx[None, :], c, jnp.zeros((SUBLANES_F32 - 1 - dec_batch, D_MODEL), F32)], axis=0)
    mods = _mod_call(cond, w_mod, b_mod)

    w_main, w_dt, w_gates = _split_w_in(w_in)
    wa = _cast_call(w_attn_o, D_MODEL, D_MODEL)
    ws = _cast_call(w_ssd_o, D_MODEL, D_MODEL // 2)
    wo = _cast_call(w_out, D_MODEL, D_MODEL)
    wu = _cast_call(w_up, 2 * D_FF, D_FF // 2)
    wd = _cast_call(w_down, D_MODEL, D_MODEL // 2)
    qn = q_norm.reshape(depth, 1, HEAD_DIM)
    kn = k_norm.reshape(depth, 1, HEAD_DIM)
    cvb = conv_b.reshape(depth, 1, CONV_CH)
    fcb = ffn_conv_b.reshape(depth, 1, 2 * D_FF)
    dtb = _dir_lanes(dt_bias).reshape(depth, 1, 2 * LANES)
    alog = _dir_lanes(a_log).reshape(depth, 2, 1, LANES)
    dsk = jnp.repeat(d_skip, SSM_HEAD_DIM, axis=-1).reshape(depth, 1, D_INNER)
    nrm = ssd_norm.reshape(depth, 1, D_INNER)
    ck = cache_k.reshape(dec_batch, depth, past, KV_W)
    cv = cache_v.reshape(dec_batch, depth, past, KV_W)

    x = (x_prompt.reshape(n_ctx, D_MODEL), x_sample.reshape(dec_batch * dec_seq, D_MODEL))
    new_k = jnp.zeros((batch, depth, seq, H_KV, HEAD_DIM), F32)
    new_v = jnp.zeros((batch, depth, seq, H_KV, HEAD_DIM), F32)
    new_s = jnp.zeros((batch, depth, 2, SSM_HEADS * SSM_HEAD_DIM, SSM_STATE), F32)
    for l in range(depth):
        q, k, v, kf, vf, z, xbc, dt, ga, gs = _in_call(
            l, x, mods, tabs, rope_cos, rope_sin, qn, kn, conv_w, cvb, dtb, w_main, w_dt, w_gates)
        attn_c, new_k, new_v = _attn_ctx_call(l, q, k, v, kf, vf, new_k, new_v, batch, seq)
        attn_l = _attn_lat_call(q, k, v, ck, cv, n_ctx, dec_batch, dec_seq, l)
        yf, yb, new_s = _ssd_call(l, xbc, dt, alog, state_ssd, tabs, new_s)
        x1, h2 = _comb_call(l, tabs, attn_c, attn_l, yf, yb, xbc, z, ga, gs, x, mods, dsk, nrm, wa, ws, wo)
        x = _ffn_call(l, tabs, h2, x1, mods, ffn_conv_w, fcb, wu, wd, n_ctx, split_out=(l == depth - 1))
    y_prompt = x[0].reshape(batch, seq, D_MODEL)
    y_sample = x[1].reshape(dec_batch, dec_seq, D_MODEL)
    new_state = new_s.reshape(batch, depth, 2, SSM_HEADS, SSM_HEAD_DIM, SSM_STATE)
    return (y_prompt, y_sample, new_k, new_v, new_state)
```
